```python
import math
import jax
import jax.numpy as jnp
from jax import lax
import numpy as np


D_MODEL = 1024
BATCH = 8
SEQ = 2048
DEPTH = 4

GRID_W = 64
CTX_LEN = 256
N_MIXERS = 3
HEAD_DIM = 64
ROPE_QUARTER = HEAD_DIM // 4
ROPE_BASE = 10000.0
EPS = 1e-6
NEG = -1e30
DA_HEADS = 8
DA_HEAD_DIM = HEAD_DIM
DA_IN_DIM = 3 * DA_HEADS * 2 * DA_HEAD_DIM
Q_BLOCK = 128
ML_HEADS = 8
ML_QK_DIM = 64
ML_V_DIM = D_MODEL // ML_HEADS
ML_CHUNK = 64
ML_FORGET_BIAS = 3.0
ML_IN_DIM = 2 * ML_HEADS * ML_QK_DIM + 2 * ML_HEADS * ML_V_DIM + 4 * ML_HEADS
SW_Q_HEADS = 16
SW_KV_HEADS = 4
SW_HEAD_DIM = HEAD_DIM
SW_WINDOW = 128
SW_BLOCK = 128
SW_IN_DIM = (SW_Q_HEADS + 2 * SW_KV_HEADS) * SW_HEAD_DIM
FFN_DIM = 2816
FFN_CONV = 3

kernel_name = 'hybrid_diffusion_interleaved_block'


def rmsnorm(x, g):
    xf = x.astype(jnp.float32)
    y = xf * lax.rsqrt(jnp.mean(xf * xf, axis=-1, keepdims=True) + EPS)
    return (y * g.astype(jnp.float32)).astype(x.dtype)


def modulate(h, shift, scale):
    return h * (1.0 + scale) + shift


def axial_rope_tables(n_tokens):
    rows = n_tokens // GRID_W
    row = jnp.repeat(jnp.arange(rows, dtype=jnp.float32), GRID_W)
    col = jnp.tile(jnp.arange(GRID_W, dtype=jnp.float32), rows)
    inv = ROPE_BASE ** (-jnp.arange(ROPE_QUARTER, dtype=jnp.float32) / ROPE_QUARTER)
    ang_r = row[:, None] * inv
    ang_c = col[:, None] * inv
    return (jnp.cos(ang_r)[:, None, :], jnp.sin(ang_r)[:, None, :],
            jnp.cos(ang_c)[:, None, :], jnp.sin(ang_c)[:, None, :])


def rope_half(x, cos, sin):
    x1, x2 = jnp.split(x, 2, axis=-1)
    cos = cos.astype(x.dtype)
    sin = sin.astype(x.dtype)
    return jnp.concatenate([x1 * cos - x2 * sin, x2 * cos + x1 * sin], axis=-1)


def apply_axial_rope(x, rope):
    cr, sr, cc, sc = rope
    half = x.shape[-1] // 2
    return jnp.concatenate([rope_half(x[..., :half], cr, sr), rope_half(x[..., half:], cc, sc)], axis=-1)


def diff_core(q, k, v, lam):
    s = jnp.einsum('bqhcd,bkhcd->bhcqk', q, k).astype(jnp.float32) * (DA_HEAD_DIM ** -0.5)
    p = jax.nn.softmax(s, axis=-1)
    w = p[:, :, 0] - lam * p[:, :, 1]
    return jnp.einsum('bhqk,bkhe->bqhe', w.astype(v.dtype), v)


def diff_attention(h_lat, h_ctx, w_in, lam_q1, lam_k1, lam_q2, lam_k2, subln_g, w_out, layer_idx, rope, need_ctx):
    f32 = jnp.float32
    lam_init = 0.8 - 0.6 * math.exp(-0.3 * layer_idx)
    lam = (jnp.exp(jnp.sum(lam_q1.astype(f32) * lam_k1.astype(f32)))
           - jnp.exp(jnp.sum(lam_q2.astype(f32) * lam_k2.astype(f32))) + lam_init)

    def project(h, use_rope):
        b, t, _ = h.shape
        q, k, v = jnp.split(h @ w_in, 3, axis=-1)
        q = q.reshape(b, t, 2 * DA_HEADS, DA_HEAD_DIM)
        k = k.reshape(b, t, 2 * DA_HEADS, DA_HEAD_DIM)
        if use_rope:
            q = apply_axial_rope(q, rope)
            k = apply_axial_rope(k, rope)
        return (q.reshape(b, t, DA_HEADS, 2, DA_HEAD_DIM),
                k.reshape(b, t, DA_HEADS, 2, DA_HEAD_DIM),
                v.reshape(b, t, DA_HEADS, 2 * DA_HEAD_DIM))

    def finish(o):
        b, t = o.shape[:2]
        return (rmsnorm(o, subln_g) * (1.0 - lam_init)).reshape(b, t, -1) @ w_out

    ql, kl, vl = project(h_lat, True)
    qc, kc, vc = project(h_ctx, False)
    k_all = jnp.concatenate([kc, kl], axis=1)
    v_all = jnp.concatenate([vc, vl], axis=1)
    b, s = h_lat.shape[:2]
    nb = s // Q_BLOCK
    q_blocks = jnp.moveaxis(ql.reshape(b, nb, Q_BLOCK, DA_HEADS, 2, DA_HEAD_DIM), 1, 0)
    o_blocks = lax.map(lambda qb: diff_core(qb, k_all, v_all, lam), q_blocks)
    o_lat = jnp.moveaxis(o_blocks, 0, 1).reshape(b, s, DA_HEADS, 2 * DA_HEAD_DIM)
    out_ctx = finish(diff_core(qc, kc, vc, lam)) if need_ctx else None
    return finish(o_lat), out_ctx


def mlstm_chunkwise(q, k, v, ig, lf, state):
    b, h, t, _ = q.shape
    nc = t // ML_CHUNK

    def to_chunks(a):
        return jnp.moveaxis(a.reshape(a.shape[:2] + (nc, ML_CHUNK) + a.shape[3:]), 2, 0)

    tril = jnp.tril(jnp.ones((ML_CHUNK, ML_CHUNK), dtype=bool))

    def step(carry, xs):
        c_prev, n_prev, m_prev = carry
        qc, kc, vc, ic, fc = xs
        bcum = jnp.cumsum(fc, axis=-1)
        dmat = jnp.where(tril, bcum[..., :, None] - bcum[..., None, :] + ic[..., None, :], -jnp.inf)
        m_t = jnp.maximum(bcum + m_prev[..., None], jnp.max(dmat, axis=-1))
        inter = jnp.exp(bcum + m_prev[..., None] - m_t)
        s = jnp.einsum('bhtd,bhsd->bhts', qc, kc) * jnp.exp(dmat - m_t[..., None])
        num = inter[..., None] * jnp.einsum('bhed,bhtd->bhte', c_prev, qc) + jnp.einsum('bhts,bhse->bhte', s, vc)
        den = inter * jnp.einsum('bhd,bhtd->bht', n_prev, qc) + jnp.sum(s, axis=-1)
        h_out = num / jnp.maximum(jnp.abs(den), jnp.exp(-m_t))[..., None]
        m_new = m_t[..., -1]
        g = jnp.exp(bcum[..., -1:] - bcum + ic - m_new[..., None])
        decay = jnp.exp(bcum[..., -1] + m_prev - m_new)
        c_new = decay[..., None, None] * c_prev + jnp.einsum('bhs,bhse,bhsd->bhed', g, vc, kc)
        n_new = decay[..., None] * n_prev + jnp.einsum('bhs,bhsd->bhd', g, kc)
        return (c_new, n_new, m_new), h_out

    state, hs = lax.scan(step, state, (to_chunks(q), to_chunks(k), to_chunks(v), to_chunks(ig), to_chunks(lf)))
    return jnp.moveaxis(hs, 0, 2).reshape(b, h, t, v.shape[-1]), state


def mlstm_mixer(h_lat, h_ctx, w_in, gate_b, norm_g, w_out, need_ctx):
    f32 = jnp.float32
    nqk = ML_HEADS * ML_QK_DIM
    nv = ML_HEADS * ML_V_DIM

    def project(h):
        b, t, _ = h.shape
        y = h @ w_in
        heads = lambda a, d: jnp.swapaxes(a.reshape(b, t, ML_HEADS, d), 1, 2).astype(f32)
        q = heads(y[..., :nqk], ML_QK_DIM) * (ML_QK_DIM ** -0.5)
        k = heads(y[..., nqk:2 * nqk], ML_QK_DIM)
        v = heads(y[..., 2 * nqk:2 * nqk + nv], ML_V_DIM)
        o = y[..., 2 * nqk + nv:2 * nqk + 2 * nv]
        g = (y[..., 2 * nqk + 2 * nv:] + gate_b).astype(f32).reshape(b, t, 4, ML_HEADS)
        g = jnp.moveaxis(g, 1, 3)
        return (q, k, v, o, g[:, 0], jax.nn.log_sigmoid(g[:, 1]), g[:, 2], jax.nn.log_sigmoid(g[:, 3]))

    def finish(hsum, o):
        b, t = o.shape[:2]
        hh = rmsnorm(jnp.swapaxes(hsum, 1, 2).astype(o.dtype), norm_g.reshape(ML_HEADS, ML_V_DIM))
        hh = hh * jax.nn.sigmoid(o).reshape(b, t, ML_HEADS, ML_V_DIM)
        return hh.reshape(b, t, nv) @ w_out

    rev = lambda a: jnp.flip(a, axis=2)
    qc, kc, vc, oc, icf, lcf, icb, lcb = project(h_ctx)
    ql, kl, vl, ol, ilf, llf, ilb, llb = project(h_lat)
    b = h_lat.shape[0]
    state0 = (jnp.zeros((b, ML_HEADS, ML_V_DIM, ML_QK_DIM), f32),
              jnp.zeros((b, ML_HEADS, ML_QK_DIM), f32),
              jnp.zeros((b, ML_HEADS), f32))
    hcf, st_f = mlstm_chunkwise(qc, kc, vc, icf, lcf, state0)
    hlf, _ = mlstm_chunkwise(ql, kl, vl, ilf, llf, st_f)
    hcb, st_b = mlstm_chunkwise(rev(qc), rev(kc), rev(vc), rev(icb), rev(lcb), state0)
    hlb, _ = mlstm_chunkwise(rev(ql), rev(kl), rev(vl), rev(ilb), rev(llb), st_b)
    out_lat = finish(hlf + rev(hlb), ol)
    out_ctx = finish(hcf + rev(hcb), oc) if need_ctx else None
    return out_lat, out_ctx


def sink_attention(q, k, v, sink_f, mask):
    s = jnp.einsum('bqhgd,bkhd->bhgqk', q, k).astype(jnp.float32) * (SW_HEAD_DIM ** -0.5)
    if mask is not None:
        s = jnp.where(mask[None, None, None], s, NEG)
    sink_col = jnp.broadcast_to(sink_f[None, :, :, None, None], s.shape[:-1] + (1,))
    p = jax.nn.softmax(jnp.concatenate([s, sink_col], axis=-1), axis=-1)[..., :-1]
    return jnp.einsum('bhgqk,bkhd->bqhgd', p.astype(v.dtype), v)


def swa_attention(h_lat, h_ctx, w_in, sink, w_out, rope, need_ctx):
    hq, hkv, dh = SW_Q_HEADS, SW_KV_HEADS, SW_HEAD_DIM
    grp = hq // hkv
    sink_f = sink.astype(jnp.float32).reshape(hkv, grp)

    def project(h):
        b, t, _ = h.shape
        y = h @ w_in
        q = y[..., :hq * dh].reshape(b, t, hq, dh)
        k = y[..., hq * dh:(hq + hkv) * dh].reshape(b, t, hkv, dh)
        v = y[..., (hq + hkv) * dh:].reshape(b, t, hkv, dh)
        return q, k, v

    ql, kl, vl = project(h_lat)
    ql = apply_axial_rope(ql, rope)
    kl = apply_axial_rope(kl, rope)
    qc, kc, vc = project(h_ctx)
    b, s = h_lat.shape[:2]
    n_ctx = kc.shape[1]
    nb = s // SW_BLOCK

    def band(a):
        ap = jnp.pad(a, ((0, 0), (SW_BLOCK, SW_BLOCK), (0, 0), (0, 0))).reshape(b, nb + 2, SW_BLOCK, hkv, dh)
        return jnp.moveaxis(jnp.concatenate([ap[:, :-2], ap[:, 1:-1], ap[:, 2:]], axis=2), 1, 0)

    kb, vb = band(kl), band(vl)
    qb = jnp.moveaxis(ql.reshape(b, nb, SW_BLOCK, hkv, grp, dh), 1, 0)
    start = jnp.arange(nb)[:, None] * SW_BLOCK
    qpos = start + jnp.arange(SW_BLOCK)[None]
    kpos = start - SW_BLOCK + jnp.arange(3 * SW_BLOCK)[None]
    band_mask = ((jnp.abs(qpos[:, :, None] - kpos[:, None, :]) <= SW_WINDOW)
                 & (kpos[:, None, :] >= 0) & (kpos[:, None, :] < s))
    ctx_mask = jnp.ones((nb, SW_BLOCK, n_ctx), dtype=bool)
    full_mask = jnp.concatenate([ctx_mask, band_mask], axis=-1)

    def block(args):
        q_j, k_j, v_j, m_j = args
        return sink_attention(q_j, jnp.concatenate([kc, k_j], axis=1), jnp.concatenate([vc, v_j], axis=1), sink_f, m_j)

    o = lax.map(block, (qb, kb, vb, full_mask))
    out_lat = jnp.moveaxis(o, 0, 1).reshape(b, s, hq * dh) @ w_out
    out_ctx = None
    if need_ctx:
        oc = sink_attention(qc.reshape(b, n_ctx, hkv, grp, dh), kc, vc, sink_f, None)
        out_ctx = oc.reshape(b, n_ctx, hq * dh) @ w_out
    return out_lat, out_ctx


def conv_ffn(h, w_up, conv_w, conv_b, w_down):
    u = h @ w_up
    t = u.shape[1]
    pad = (FFN_CONV - 1) // 2
    up = jnp.pad(u, ((0, 0), (pad, pad), (0, 0)))
    u = conv_b + sum(up[:, j:j + t] * conv_w[j] for j in range(FFN_CONV))
    a, g = jnp.split(u, 2, axis=-1)
    return (a * jax.nn.silu(g)) @ w_down


def setup_inputs(seed: int = 0) -> dict:
    key = jax.random.key(seed)
    keys = iter(jax.random.split(key, 128))
    f32 = jnp.float32

    def normal(shape, scale=1.0):
        return scale * jax.random.normal(next(keys), shape, f32)

    def dense(fan_in, fan_out, gain=1.0):
        return normal((fan_in, fan_out), gain * fan_in ** -0.5)

    def norm_gain(n):
        return 1.0 + normal((n,), 0.1)

    def bias(n):
        return normal((n,), 0.02)

    d = D_MODEL
    inputs = {
        'x': normal((BATCH, SEQ, d)),
        'c': normal((BATCH, d)),
        'ctx': normal((BATCH, CTX_LEN, d)),
        'c_ctx': normal((d,)),
    }
    for i in range(DEPTH):
        p = 'l%d_' % i
        kind = i % N_MIXERS
        inputs[p + 'ada_w'] = dense(d, 6 * d, 0.5)
        inputs[p + 'ada_b'] = bias(6 * d)
        inputs[p + 'norm1_g'] = norm_gain(d)
        if kind == 0:
            inputs[p + 'da_w_in'] = dense(d, DA_IN_DIM)
            for nm in ('q1', 'k1', 'q2', 'k2'):
                inputs[p + 'da_lam_' + nm] = normal((DA_HEAD_DIM,), 0.1)
            inputs[p + 'da_subln_g'] = norm_gain(2 * DA_HEAD_DIM)
            inputs[p + 'da_w_out'] = dense(d, d)
        elif kind == 1:
            inputs[p + 'ml_w_in'] = dense(d, ML_IN_DIM)
            forget_offset = jnp.tile(jnp.repeat(jnp.array([0.0, ML_FORGET_BIAS], f32), ML_HEADS), 2)
            inputs[p + 'ml_gate_b'] = forget_offset + normal((4 * ML_HEADS,), 0.1)
            inputs[p + 'ml_norm_g'] = norm_gain(ML_HEADS * ML_V_DIM)
            inputs[p + 'ml_w_out'] = dense(ML_HEADS * ML_V_DIM, d)
        else:
            inputs[p + 'sw_w_in'] = dense(d, SW_IN_DIM)
            inputs[p + 'sw_sink'] = normal((SW_Q_HEADS,), 0.5)
            inputs[p + 'sw_w_out'] = dense(SW_Q_HEADS * SW_HEAD_DIM, d)
        inputs[p + 'norm2_g'] = norm_gain(d)
        inputs[p + 'ffn_w_up'] = dense(d, 2 * FFN_DIM)
        inputs[p + 'ffn_conv_w'] = normal((FFN_CONV, 2 * FFN_DIM), FFN_CONV ** -0.5)
        inputs[p + 'ffn_conv_b'] = bias(2 * FFN_DIM)
        inputs[p + 'ffn_w_down'] = dense(FFN_DIM, d)
    inputs['final_norm_g'] = norm_gain(d)
    return inputs


def reference(x, c, ctx, c_ctx,
              l0_ada_w, l0_ada_b, l0_norm1_g, l0_da_w_in, l0_da_lam_q1, l0_da_lam_k1, l0_da_lam_q2, l0_da_lam_k2,
              l0_da_subln_g, l0_da_w_out, l0_norm2_g, l0_ffn_w_up, l0_ffn_conv_w, l0_ffn_conv_b, l0_ffn_w_down,
              l1_ada_w, l1_ada_b, l1_norm1_g, l1_ml_w_in, l1_ml_gate_b, l1_ml_norm_g, l1_ml_w_out,
              l1_norm2_g, l1_ffn_w_up, l1_ffn_conv_w, l1_ffn_conv_b, l1_ffn_w_down,
              l2_ada_w, l2_ada_b, l2_norm1_g, l2_sw_w_in, l2_sw_sink, l2_sw_w_out,
              l2_norm2_g, l2_ffn_w_up, l2_ffn_conv_w, l2_ffn_conv_b, l2_ffn_w_down,
              l3_ada_w, l3_ada_b, l3_norm1_g, l3_da_w_in, l3_da_lam_q1, l3_da_lam_k1, l3_da_lam_q2, l3_da_lam_k2,
              l3_da_subln_g, l3_da_w_out, l3_norm2_g, l3_ffn_w_up, l3_ffn_conv_w, l3_ffn_conv_b, l3_ffn_w_down,
              final_norm_g):
    layers = (
        (l0_ada_w, l0_ada_b, l0_norm1_g,
         (l0_da_w_in, l0_da_lam_q1, l0_da_lam_k1, l0_da_lam_q2, l0_da_lam_k2, l0_da_subln_g, l0_da_w_out),
         l0_norm2_g, (l0_ffn_w_up, l0_ffn_conv_w, l0_ffn_conv_b, l0_ffn_w_down)),
        (l1_ada_w, l1_ada_b, l1_norm1_g,
         (l1_ml_w_in, l1_ml_gate_b, l1_ml_norm_g, l1_ml_w_out),
         l1_norm2_g, (l1_ffn_w_up, l1_ffn_conv_w, l1_ffn_conv_b, l1_ffn_w_down)),
        (l2_ada_w, l2_ada_b, l2_norm1_g,
         (l2_sw_w_in, l2_sw_sink, l2_sw_w_out),
         l2_norm2_g, (l2_ffn_w_up, l2_ffn_conv_w, l2_ffn_conv_b, l2_ffn_w_down)),
        (l3_ada_w, l3_ada_b, l3_norm1_g,
         (l3_da_w_in, l3_da_lam_q1, l3_da_lam_k1, l3_da_lam_q2, l3_da_lam_k2, l3_da_subln_g, l3_da_w_out),
         l3_norm2_g, (l3_ffn_w_up, l3_ffn_conv_w, l3_ffn_conv_b, l3_ffn_w_down)),
    )
    rope = axial_rope_tables(x.shape[1])
    for i in range(DEPTH):
        ada_w, ada_b, norm1_g, mix, norm2_g, ffn = layers[i]
        kind = i % N_MIXERS
        need_ctx = i < DEPTH - 1
        mod_l = jnp.split((jax.nn.silu(c) @ ada_w + ada_b)[:, None, :], 6, axis=-1)
        mod_c = jnp.split((jax.nn.silu(c_ctx) @ ada_w + ada_b)[None, None, :], 6, axis=-1)
        h_lat = modulate(rmsnorm(x, norm1_g), mod_l[0], mod_l[1])
        h_ctx = modulate(rmsnorm(ctx, norm1_g), mod_c[0], mod_c[1])
        if kind == 0:
            o_lat, o_ctx = diff_attention(h_lat, h_ctx, *mix, i, rope, need_ctx)
        elif kind == 1:
            o_lat, o_ctx = mlstm_mixer(h_lat, h_ctx, *mix, need_ctx)
        else:
            o_lat, o_ctx = swa_attention(h_lat, h_ctx, *mix, rope, need_ctx)
        x = x + mod_l[2] * o_lat
        x = x + mod_l[5] * conv_ffn(modulate(rmsnorm(x, norm2_g), mod_l[3], mod_l[4]), *ffn)
        if need_ctx:
            ctx = ctx + mod_c[2] * o_ctx
            ctx = ctx + mod_c[5] * conv_ffn(modulate(rmsnorm(ctx, norm2_g), mod_c[3], mod_c[4]), *ffn)
    return rmsnorm(x, final_norm_g)
```

```python
import functools
import math

import jax
import jax.numpy as jnp
from jax import lax
from jax.experimental import pallas as pl
from jax.experimental.pallas import tpu as pltpu

F32 = jnp.float32
BF16 = jnp.bfloat16
HIGHEST = lax.Precision.HIGHEST

D = 1024
B = 8
S = 2048
CTX = 256
T = CTX + S
DEPTH = 4
GRID_W = 64
HEAD_DIM = 64
EPS = 1e-6
NEG = -1e30

TM = 256
NT = T // TM
NS = S // TM
MOD_ROWS = 16

DA_HEADS = 8
ML_HEADS = 8
ML_V = 128
SW_KV = 4
SW_GRP = 4
SW_WIN = 128
SW_QB = 128
FFN = 2816
FFN_CHUNK = 256
HALO = 8

NT_DIMS = (((1,), (1,)), ((), ()))
TN_DIMS = (((0,), (0,)), ((), ()))

VMEM_LIMIT = 56 * 1024 * 1024


def _cparams(*sem):
    return pltpu.CompilerParams(dimension_semantics=sem, vmem_limit_bytes=VMEM_LIMIT)


def _resident(shape):
    zeros = (0,) * len(shape)
    return pl.BlockSpec(shape, lambda *_: zeros, pipeline_mode=pl.Buffered(1))


def _rms(x, g):
    ms = jnp.mean(x * x, axis=-1, keepdims=True)
    return x * lax.rsqrt(ms + EPS) * g


def _norm_mod(x, g, shift, scale):
    return _rms(x, g) * (1.0 + scale) + shift


def _mod_ix_ctx(b, i):
    return (jnp.where(i == 0, B, b), 0, 0)


def _mod_ix_lat(b, i):
    return (b, 0, 0)


def _ada_kernel(c_ref, w_ref, b_ref, o_ref):
    c = c_ref[...]
    a = (c * jax.nn.sigmoid(c)).astype(BF16)
    o_ref[...] = jnp.dot(a, w_ref[...].astype(BF16), preferred_element_type=F32) + b_ref[...]


def _ada(cpad, w, bias):
    n = w.shape[1]
    tn = 1024
    return pl.pallas_call(
        _ada_kernel,
        grid=(n // tn,),
        in_specs=[pl.BlockSpec((MOD_ROWS, D), lambda j: (0, 0)),
                  pl.BlockSpec((D, tn), lambda j: (0, j)),
                  pl.BlockSpec((1, tn), lambda j: (0, j))],
        out_specs=pl.BlockSpec((MOD_ROWS, tn), lambda j: (0, j)),
        out_shape=jax.ShapeDtypeStruct((MOD_ROWS, n), F32),
        compiler_params=_cparams("arbitrary"),
        name="ada",
    )(cpad, w, bias.reshape(1, n))


def _rope(c, cos, sa, sb):
    return c * cos + pltpu.roll(c, 112, 1) * sa + pltpu.roll(c, 16, 1) * sb


def _attn_proj_kernel(x_ref, mod_ref, g_ref, w_ref, cos_ref, sa_ref, sb_ref, o_ref, *, nq, nk, nv):
    h = _norm_mod(x_ref[...], g_ref[...], mod_ref[0:1, :], mod_ref[1:2, :]).astype(BF16)
    cos, sa, sb = cos_ref[...], sa_ref[...], sb_ref[...]
    for j in range(nq + nk + nv):
        y = jnp.dot(h, w_ref[:, j * 256:(j + 1) * 256], preferred_element_type=F32)
        if j < nq + nk:
            halves = []
            for t in range(2):
                r = _rope(y[:, t * 128:(t + 1) * 128], cos, sa, sb)
                halves.append(r * 0.125 if j < nq else r)
            y = jnp.concatenate(halves, axis=1)
        o_ref[:, j * 256:(j + 1) * 256] = y.astype(BF16)


def _attn_proj(xs, mod, g, w, rope, nq, nk, nv):
    n = (nq + nk + nv) * 256
    cos, sa, sb = rope
    tab = pl.BlockSpec((TM, 128), lambda b, i: (i, 0))
    return pl.pallas_call(
        functools.partial(_attn_proj_kernel, nq=nq, nk=nk, nv=nv),
        grid=(B, NT),
        in_specs=[pl.BlockSpec((None, TM, D), lambda b, i: (b, i, 0)),
                  pl.BlockSpec((None, 6, D), _mod_ix_ctx),
                  _resident((1, D)),
                  _resident((D, n)),
                  tab, tab, tab],
        out_specs=pl.BlockSpec((None, TM, n), lambda b, i: (b, i, 0)),
        out_shape=jax.ShapeDtypeStruct((B, T, n), BF16),
        compiler_params=_cparams("parallel", "parallel"),
        name="attn_proj",
    )(xs, mod, g, w, cos, sa, sb)


def _ml_proj_kernel(x_ref, mod_ref, g_ref, w_ref, wg_ref, wgt_ref, gb_ref, gbt_ref,
                    qk_ref, v_ref, o_ref, gc_ref, gr_ref):
    h = _norm_mod(x_ref[...], g_ref[...], mod_ref[0:1, :], mod_ref[1:2, :]).astype(BF16)
    for j in range(12):
        y = jnp.dot(h, w_ref[:, j * 256:(j + 1) * 256], preferred_element_type=F32)
        if j < 2:
            qk_ref[:, j * 256:(j + 1) * 256] = (y * 0.125).astype(BF16)
        elif j < 4:
            qk_ref[:, j * 256:(j + 1) * 256] = y.astype(BF16)
        elif j < 8:
            v_ref[:, (j - 4) * 256:(j - 3) * 256] = y.astype(BF16)
        else:
            o_ref[:, (j - 8) * 256:(j - 7) * 256] = y
    gc_ref[...] = jnp.dot(h, wg_ref[...], preferred_element_type=F32) + gb_ref[...]
    gr_ref[...] = lax.dot_general(wgt_ref[...], h, NT_DIMS, preferred_element_type=F32) + gbt_ref[...]


def _ml_proj(xs, mod, g, w, wg, wgt, gb, gbt):
    tile = lambda n, dt: (pl.BlockSpec((None, TM, n), lambda b, i: (b, i, 0)),
                          jax.ShapeDtypeStruct((B, T, n), dt))
    outs = [tile(D, BF16), tile(D, BF16), tile(D, F32), tile(32, F32),
            (pl.BlockSpec((None, 32, TM), lambda b, i: (b, 0, i)), jax.ShapeDtypeStruct((B, 32, T), F32))]
    return pl.pallas_call(
        _ml_proj_kernel,
        grid=(B, NT),
        in_specs=[pl.BlockSpec((None, TM, D), lambda b, i: (b, i, 0)),
                  pl.BlockSpec((None, 6, D), _mod_ix_ctx),
                  _resident((1, D)),
                  _resident((D, 3072)),
                  _resident((D, 32)),
                  _resident((32, D)),
                  _resident((1, 32)),
                  _resident((32, 1))],
        out_specs=[o[0] for o in outs],
        out_shape=[o[1] for o in outs],
        compiler_params=_cparams("parallel", "parallel"),
        name="ml_proj",
    )(xs, mod, g, w, wg, wgt, gb, gbt)


def _da_attn_kernel(lam_ref, g_ref, q_ref, k_ref, v_ref, o_ref, *, lam_init, has_ctx_tile):
    i = pl.program_id(2)
    lv = lam_ref[...]
    lam = (jnp.exp(jnp.sum(lv[0:1] * lv[1:2], axis=-1, keepdims=True))
           - jnp.exp(jnp.sum(lv[2:3] * lv[3:4], axis=-1, keepdims=True)) + lam_init)
    lane = lax.broadcasted_iota(jnp.int32, (TM, 128), 1)
    q = q_ref[...]
    zero = jnp.zeros_like(q)
    qc = (jnp.where(lane < HEAD_DIM, q, zero), jnp.where(lane >= HEAD_DIM, q, zero))

    def attend(nk):
        k = k_ref[0:nk, :]
        v = v_ref[0:nk, :]
        w = None
        for c in range(2):
            s = lax.dot_general(qc[c], k, NT_DIMS, preferred_element_type=F32)
            e = jnp.exp(s - jnp.max(s, axis=-1, keepdims=True))
            r = 1.0 / jnp.sum(e, axis=-1, keepdims=True)
            w = e * r if c == 0 else w - e * (lam * r)
        o = jnp.dot(w.astype(BF16), v, preferred_element_type=F32)
        o_ref[...] = (_rms(o, g_ref[...]) * (1.0 - lam_init)).astype(BF16)

    if has_ctx_tile:
        pl.when(i == 0)(lambda: attend(CTX))
        pl.when(i > 0)(lambda: attend(T))
    else:
        attend(T)


def _da_attn(qkv, lamv, subln_g, lam_init, need_ctx):
    off = 0 if need_ctx else 1
    nq = NT - off
    return pl.pallas_call(
        functools.partial(_da_attn_kernel, lam_init=lam_init, has_ctx_tile=need_ctx),
        grid=(B, DA_HEADS, nq),
        in_specs=[_resident((4, HEAD_DIM)),
                  _resident((1, 128)),
                  pl.BlockSpec((None, TM, 128), lambda b, h, i: (b, i + off, h)),
                  pl.BlockSpec((None, T, 128), lambda b, h, i: (b, 0, DA_HEADS + h)),
                  pl.BlockSpec((None, T, 128), lambda b, h, i: (b, 0, 2 * DA_HEADS + h))],
        out_specs=pl.BlockSpec((None, TM, 128), lambda b, h, i: (b, i, h)),
        out_shape=jax.ShapeDtypeStruct((B, nq * TM, D), BF16),
        compiler_params=_cparams("parallel", "parallel", "parallel"),
        name="da_attn",
    )(lamv, subln_g, qkv, qkv, qkv)


def _swa_kernel(sink_ref, q_ref, k_ref, v_ref, o_ref):
    blk = pl.program_id(1)
    nwin = 3 * SW_QB
    lane_q = lax.broadcasted_iota(jnp.int32, (SW_QB, 256), 1) // HEAD_DIM

    def body(is_lat):
        kc = k_ref[0:CTX, :]
        vc = v_ref[0:CTX, :]
        lane_c = lax.broadcasted_iota(jnp.int32, (CTX, 256), 1) // HEAD_DIM
        if is_lat:
            j = blk - CTX // SW_QB
            start = pl.multiple_of(jnp.minimum(CTX + (j - 1) * SW_QB, T - nwin), SW_QB)
            kw = k_ref[pl.ds(start, nwin), :]
            vw = v_ref[pl.ds(start, nwin), :]
            lane_w = lax.broadcasted_iota(jnp.int32, (nwin, 256), 1) // HEAD_DIM
            qpos = j * SW_QB + lax.broadcasted_iota(jnp.int32, (SW_QB, nwin), 0)
            kpos = start - CTX + lax.broadcasted_iota(jnp.int32, (SW_QB, nwin), 1)
            band = (jnp.abs(qpos - kpos) <= SW_WIN) & (kpos >= 0)
        acc = [jnp.zeros((SW_QB, 256), F32) for _ in range(SW_GRP)]
        for h in range(SW_KV):
            kcz = jnp.where(lane_c == h, kc, jnp.zeros_like(kc))
            if is_lat:
                kwz = jnp.where(lane_w == h, kw, jnp.zeros_like(kw))
            for g in range(SW_GRP):
                qg = q_ref[:, g * 256:(g + 1) * 256]
                sink = sink_ref[0:1, SW_GRP * h + g:SW_GRP * h + g + 1]
                s_c = lax.dot_general(qg, kcz, NT_DIMS, preferred_element_type=F32)
                m = jnp.maximum(jnp.max(s_c, axis=-1, keepdims=True), sink)
                if is_lat:
                    s_w = lax.dot_general(qg, kwz, NT_DIMS, preferred_element_type=F32)
                    s_w = jnp.where(band, s_w, NEG)
                    m = jnp.maximum(m, jnp.max(s_w, axis=-1, keepdims=True))
                e_c = jnp.exp(s_c - m)
                den = jnp.sum(e_c, axis=-1, keepdims=True) + jnp.exp(sink - m)
                if is_lat:
                    e_w = jnp.exp(s_w - m)
                    den = den + jnp.sum(e_w, axis=-1, keepdims=True)
                r = 1.0 / den
                o = jnp.dot((e_c * r).astype(BF16), vc, preferred_element_type=F32)
                if is_lat:
                    o = o + jnp.dot((e_w * r).astype(BF16), vw, preferred_element_type=F32)
                acc[g] = jnp.where(lane_q == h, o, acc[g])
        for g in range(SW_GRP):
            o_ref[:, g * 256:(g + 1) * 256] = acc[g].astype(BF16)

    pl.when(blk < CTX // SW_QB)(lambda: body(False))
    pl.when(blk >= CTX // SW_QB)(lambda: body(True))


def _swa_attn(qkv, sink):
    return pl.pallas_call(
        _swa_kernel,
        grid=(B, T // SW_QB),
        in_specs=[_resident((1, 16)),
                  pl.BlockSpec((None, SW_QB, D), lambda b, i: (b, i, 0)),
                  pl.BlockSpec((None, T, 256), lambda b, i: (b, 0, 4)),
                  pl.BlockSpec((None, T, 256), lambda b, i: (b, 0, 5))],
        out_specs=pl.BlockSpec((None, SW_QB, D), lambda b, i: (b, i, 0)),
        out_shape=jax.ShapeDtypeStruct((B, T, D), BF16),
        compiler_params=_cparams("parallel", "parallel"),
        name="swa_attn",
    )(sink, qkv, qkv, qkv)


def _logsig(x):
    return jnp.minimum(x, 0.0) - jnp.log(1.0 + jnp.exp(-jnp.abs(x)))


def _mlstm_kernel(qkf_ref, vf_ref, gcf_ref, grf_ref, qkb_ref, vb_ref, gcb_ref, grb_ref,
                  hf_ref, hb_ref, st_ref, m_ref):
    step = pl.program_id(1)

    @pl.when(step == 0)
    def _():
        st_ref[...] = jnp.zeros_like(st_ref)
        m_ref[...] = jnp.zeros_like(m_ref)

    L = TM
    row = lax.broadcasted_iota(jnp.int32, (L, L), 0)
    col = lax.broadcasted_iota(jnp.int32, (L, L), 1)
    lower = col <= row
    upper = col >= row
    lower_f = lower.astype(F32)
    upper_f = upper.astype(F32)
    lane = lax.broadcasted_iota(jnp.int32, (L, 128), 1)
    ones_blk = jnp.where(lane == 0, 1.0, 0.0).astype(BF16)

    dirs = ((qkf_ref, vf_ref, gcf_ref, grf_ref, hf_ref), (qkb_ref, vb_ref, gcb_ref, grb_ref, hb_ref))
    for d, (qk_ref, v_ref, gc_ref, gr_ref, h_ref) in enumerate(dirs):
        tri = lower if d == 0 else upper
        tri_f, tri_t_f = (lower_f, upper_f) if d == 0 else (upper_f, lower_f)
        last = L - 1 if d == 0 else 0
        gc = gc_ref[...]
        gr = gr_ref[...]
        lf_c = _logsig(gc[:, 16 * d + 8:16 * d + 16])
        lf_r = _logsig(gr[16 * d + 8:16 * d + 16, :])
        cum_c = jnp.dot(tri_f, lf_c, precision=HIGHEST, preferred_element_type=F32)
        cum_r = jnp.dot(lf_r, tri_t_f, precision=HIGHEST, preferred_element_type=F32)
        for h in range(ML_HEADS):
            idx = d * ML_HEADS + h
            p = h // 2
            q2 = qk_ref[:, p * 128:(p + 1) * 128]
            k2 = qk_ref[:, 512 + p * 128:512 + (p + 1) * 128]
            own = lane >= HEAD_DIM if h % 2 else lane < HEAD_DIM
            qh = jnp.where(own, q2, jnp.zeros_like(q2))
            vh = v_ref[:, h * ML_V:(h + 1) * ML_V]
            vaug = jnp.concatenate([vh, ones_blk], axis=1)
            b_c = cum_c[:, h:h + 1]
            b_r = cum_r[h:h + 1, :]
            i_c = gc[:, 16 * d + h:16 * d + h + 1]
            i_r = gr[16 * d + h:16 * d + h + 1, :]
            m_prev = m_ref[idx][0:1, 0:1]
            st = st_ref[idx]

            dm = jnp.where(tri, b_c - b_r + i_r, NEG)
            a_c = b_c + m_prev
            m_t = jnp.maximum(a_c, jnp.max(dm, axis=-1, keepdims=True))
            inter = jnp.exp(a_c - m_t)
            sc = lax.dot_general(qh, k2, NT_DIMS, preferred_element_type=F32) * jnp.exp(dm - m_t)
            carry = jnp.dot(qh, st.astype(BF16), preferred_element_type=F32)
            num = inter * carry[:, 0:ML_V] + jnp.dot(sc.astype(BF16), vh, preferred_element_type=F32)
            den = inter * carry[:, ML_V:ML_V + 1] + jnp.sum(sc, axis=-1, keepdims=True)
            h_ref[:, h * ML_V:(h + 1) * ML_V] = num / jnp.maximum(jnp.abs(den), jnp.exp(-m_t))

            m_new = m_t[last:last + 1, :]
            b_last = b_c[last:last + 1, :]
            g_c = jnp.exp(b_last - b_c + i_c - m_new)
            decay = jnp.exp(b_last + m_prev - m_new)
            gv = (g_c * vaug.astype(F32)).astype(BF16)
            st_ref[idx] = decay * st + lax.dot_general(k2, gv, TN_DIMS, preferred_element_type=F32)
            m_ref[idx] = jnp.broadcast_to(m_new, (8, 128))


def _mlstm(qk, v, gc, gr):
    fwd = lambda b, s: (b, s, 0)
    bwd = lambda b, s: (b, jnp.where(s == 0, 0, NT - s), 0)
    fwd_t = lambda b, s: (b, 0, s)
    bwd_t = lambda b, s: (b, 0, jnp.where(s == 0, 0, NT - s))
    tok = lambda n, ix: pl.BlockSpec((None, TM, n), ix)
    return pl.pallas_call(
        _mlstm_kernel,
        grid=(B, NT),
        in_specs=[tok(D, fwd), tok(D, fwd), tok(32, fwd), pl.BlockSpec((None, 32, TM), fwd_t),
                  tok(D, bwd), tok(D, bwd), tok(32, bwd), pl.BlockSpec((None, 32, TM), bwd_t)],
        out_specs=[tok(D, fwd), tok(D, bwd)],
        out_shape=[jax.ShapeDtypeStruct((B, T, D), F32)] * 2,
        scratch_shapes=[pltpu.VMEM((2 * ML_HEADS, 128, 256), F32),
                        pltpu.VMEM((2 * ML_HEADS, 8, 128), F32)],
        compiler_params=_cparams("parallel", "arbitrary"),
        name="mlstm",
    )(qk, v, gc, gr, qk, v, gc, gr)


def _out_proj_kernel(a_ref, w_ref, x_ref, mod_ref, o_ref):
    y = jnp.dot(a_ref[...], w_ref[...], preferred_element_type=F32)
    o_ref[...] = x_ref[...] + mod_ref[2:3, :] * y


def _out_proj(a, w, xs, mod, need_ctx):
    off = 0 if need_ctx else 1
    n = NT - off
    return pl.pallas_call(
        _out_proj_kernel,
        grid=(B, n),
        in_specs=[pl.BlockSpec((None, TM, D), lambda b, i: (b, i, 0)),
                  _resident((D, D)),
                  pl.BlockSpec((None, TM, D), lambda b, i: (b, i + off, 0)),
                  pl.BlockSpec((None, 6, D), _mod_ix_ctx if need_ctx else _mod_ix_lat)],
        out_specs=pl.BlockSpec((None, TM, D), lambda b, i: (b, i, 0)),
        out_shape=jax.ShapeDtypeStruct((B, n * TM, D), F32),
        compiler_params=_cparams("parallel", "parallel"),
        name="out_proj",
    )(a, w, xs, mod)


def _ml_out_kernel(hf_ref, hb_ref, og_ref, ng_ref, w_ref, x_ref, mod_ref, o_ref):
    parts = []
    for h in range(ML_HEADS):
        sl = slice(h * ML_V, (h + 1) * ML_V)
        y = _rms(hf_ref[:, sl] + hb_ref[:, sl], ng_ref[:, sl])
        parts.append((y * jax.nn.sigmoid(og_ref[:, sl])).astype(BF16))
    y = jnp.dot(jnp.concatenate(parts, axis=1), w_ref[...], preferred_element_type=F32)
    o_ref[...] = x_ref[...] + mod_ref[2:3, :] * y


def _ml_out(hf, hb, og, ng, w, xs, mod):
    tile = pl.BlockSpec((None, TM, D), lambda b, i: (b, i, 0))
    return pl.pallas_call(
        _ml_out_kernel,
        grid=(B, NT),
        in_specs=[tile, tile, tile, _resident((1, D)), _resident((D, D)), tile,
                  pl.BlockSpec((None, 6, D), _mod_ix_ctx)],
        out_specs=tile,
        out_shape=jax.ShapeDtypeStruct((B, T, D), F32),
        compiler_params=_cparams("parallel", "parallel"),
        name="ml_out",
    )(hf, hb, og, ng, w, xs, mod)


def _ffn_kernel(x_ref, xp_ref, xn_ref, mod_ref, g_ref, wup_ref, cw_ref, cb_ref, wdn_ref, fg_ref,
                o_ref, h_scr, ua_scr, ug_scr, act_scr, *, has_ctx, nt, final):
    i = pl.program_id(1)
    g = g_ref[...]
    shift, scale, gate = mod_ref[3:4, :], mod_ref[4:5, :], mod_ref[5:6, :]
    if has_ctx:
        prev_ok = i >= 2
        next_ok = (i != 0) & (i != nt - 1)
    else:
        prev_ok = i >= 1
        next_ok = i != nt - 1
    x = x_ref[...]
    h_scr[0:HALO, :] = jnp.where(prev_ok, _norm_mod(xp_ref[...], g, shift, scale), 0.0)
    h_scr[HALO:HALO + TM, :] = _norm_mod(x, g, shift, scale)
    h_scr[HALO + TM:, :] = jnp.where(next_ok, _norm_mod(xn_ref[...], g, shift, scale), 0.0)
    hb = h_scr[...].astype(BF16)

    def conv(scr, off):
        w = cw_ref[:, off:off + FFN_CHUNK]
        return (scr[HALO - 1:HALO - 1 + TM, :] * w[0:1] + scr[HALO:HALO + TM, :] * w[1:2]
                + scr[HALO + 1:HALO + 1 + TM, :] * w[2:3] + cb_ref[:, off:off + FFN_CHUNK])

    for c in range(FFN // FFN_CHUNK):
        lo = c * FFN_CHUNK
        ua_scr[c % 2] = jnp.dot(hb, wup_ref[:, lo:lo + FFN_CHUNK], preferred_element_type=F32)
        ug_scr[c % 2] = jnp.dot(hb, wup_ref[:, FFN + lo:FFN + lo + FFN_CHUNK], preferred_element_type=F32)
        a = conv(ua_scr.at[c % 2], lo)
        gg = conv(ug_scr.at[c % 2], FFN + lo)
        act_scr[:, lo:lo + FFN_CHUNK] = (a * (gg * jax.nn.sigmoid(gg))).astype(BF16)
    y = jnp.dot(act_scr[...], wdn_ref[...], preferred_element_type=F32)
    out = x + gate * y
    if final:
        out = _rms(out, fg_ref[...])
    o_ref[...] = out


def _ffn(xs, mod, g, wup, cw, cb, wdn, fg, has_ctx, final):
    rows = xs.shape[1]
    nt = rows // TM
    per_tile = TM // HALO
    last_halo = rows // HALO - 1
    return pl.pallas_call(
        functools.partial(_ffn_kernel, has_ctx=has_ctx, nt=nt, final=final),
        grid=(B, nt),
        in_specs=[pl.BlockSpec((None, TM, D), lambda b, i: (b, i, 0)),
                  pl.BlockSpec((None, HALO, D), lambda b, i: (b, jnp.maximum(i * per_tile - 1, 0), 0)),
                  pl.BlockSpec((None, HALO, D), lambda b, i: (b, jnp.minimum((i + 1) * per_tile, last_halo), 0)),
                  pl.BlockSpec((None, 6, D), _mod_ix_ctx if has_ctx else _mod_ix_lat),
                  _resident((1, D)),
                  _resident((D, 2 * FFN)),
                  _resident((3, 2 * FFN)),
                  _resident((1, 2 * FFN)),
                  _resident((FFN, D)),
                  _resident((1, D))],
        out_specs=pl.BlockSpec((None, TM, D), lambda b, i: (b, i, 0)),
        out_shape=jax.ShapeDtypeStruct((B, rows, D), F32),
        scratch_shapes=[pltpu.VMEM((TM + 2 * HALO, D), F32),
                        pltpu.VMEM((2, TM + 2 * HALO, FFN_CHUNK), F32),
                        pltpu.VMEM((2, TM + 2 * HALO, FFN_CHUNK), F32),
                        pltpu.VMEM((TM, FFN), BF16)],
        compiler_params=_cparams("parallel", "parallel"),
        name="ffn",
    )(xs, xs, xs, mod, g, wup, cw, cb, wdn, fg)


def _rope_tables():
    rows = S // GRID_W
    quarter = HEAD_DIM // 4
    row = jnp.repeat(jnp.arange(rows, dtype=F32), GRID_W)
    col = jnp.tile(jnp.arange(GRID_W, dtype=F32), rows)
    inv = 10000.0 ** (-jnp.arange(quarter, dtype=F32) / quarter)
    ang_r = row[:, None] * inv
    ang_c = col[:, None] * inv
    cr, sr, cc, sc = jnp.cos(ang_r), jnp.sin(ang_r), jnp.cos(ang_c), jnp.sin(ang_c)
    z = jnp.zeros_like(sr)
    cos = jnp.concatenate([cr, cr, cc, cc], axis=1)
    sa = jnp.concatenate([-sr, z, -sc, z], axis=1)
    sb = jnp.concatenate([z, sr, z, sc], axis=1)
    pad = lambda t, v: jnp.concatenate([jnp.full((CTX, 128), v, F32), jnp.tile(t, (1, 2))], axis=0)
    return pad(cos, 1.0), pad(sa, 0.0), pad(sb, 0.0)


def _group_major(w, axis):
    shape = w.shape
    w = w.reshape(shape[:axis] + (SW_KV, SW_GRP, HEAD_DIM) + shape[axis + 1:])
    w = jnp.swapaxes(w, axis, axis + 1)
    return w.reshape(shape)


def _row(v):
    return v.reshape(1, -1)


def _bf(w):
    return w.astype(BF16)


def _mod(cpad, w, b):
    return _ada(cpad, w, b).reshape(MOD_ROWS, 6, D)


def _da_layer(idx, xs, mod, rope, norm1_g, w_in, lq1, lk1, lq2, lk2, subln_g, w_out):
    need_ctx = idx < DEPTH - 1
    lam_init = 0.8 - 0.6 * math.exp(-0.3 * idx)
    qkv = _attn_proj(xs, mod, _row(norm1_g), _bf(w_in), rope, 4, 4, 4)
    o = _da_attn(qkv, jnp.stack([lq1, lk1, lq2, lk2]), _row(subln_g), lam_init, need_ctx)
    return _out_proj(o, _bf(w_out), xs, mod, need_ctx)


def _ml_layer(xs, mod, norm1_g, w_in, gate_b, norm_g, w_out):
    wg = _bf(w_in[:, 3072:])
    qk, v, og, gc, gr = _ml_proj(xs, mod, _row(norm1_g), _bf(w_in[:, :3072]), wg, wg.T,
                                 _row(gate_b), gate_b.reshape(-1, 1))
    hf, hb = _mlstm(qk, v, gc, gr)
    return _ml_out(hf, hb, og, _row(norm_g), _bf(w_out), xs, mod)


def _sw_layer(xs, mod, rope, norm1_g, w_in, sink, w_out):
    w_q = _group_major(w_in[:, :D], 1)
    qkv = _attn_proj(xs, mod, _row(norm1_g), _bf(jnp.concatenate([w_q, w_in[:, D:]], axis=1)), rope, 4, 1, 1)
    o = _swa_attn(qkv, _row(sink))
    return _out_proj(o, _bf(_group_major(w_out, 0)), xs, mod, True)


def _ffn_layer(idx, xs, mod, norm2_g, w_up, conv_w, conv_b, w_down, final_norm_g):
    final = idx == DEPTH - 1
    return _ffn(xs, mod, _row(norm2_g), _bf(w_up), conv_w, _row(conv_b), _bf(w_down),
                _row(final_norm_g), has_ctx=not final, final=final)


def kernel(x, c, ctx, c_ctx, l0_ada_w, l0_ada_b, l0_norm1_g, l0_da_w_in, l0_da_lam_q1, l0_da_lam_k1, l0_da_lam_q2, l0_da_lam_k2, l0_da_subln_g, l0_da_w_out, l0_norm2_g, l0_ffn_w_up, l0_ffn_conv_w, l0_ffn_conv_b, l0_ffn_w_down, l1_ada_w, l1_ada_b, l1_norm1_g, l1_ml_w_in, l1_ml_gate_b, l1_ml_norm_g, l1_ml_w_out, l1_norm2_g, l1_ffn_w_up, l1_ffn_conv_w, l1_ffn_conv_b, l1_ffn_w_down, l2_ada_w, l2_ada_b, l2_norm1_g, l2_sw_w_in, l2_sw_sink, l2_sw_w_out, l2_norm2_g, l2_ffn_w_up, l2_ffn_conv_w, l2_ffn_conv_b, l2_ffn_w_down, l3_ada_w, l3_ada_b, l3_norm1_g, l3_da_w_in, l3_da_lam_q1, l3_da_lam_k1, l3_da_lam_q2, l3_da_lam_k2, l3_da_subln_g, l3_da_w_out, l3_norm2_g, l3_ffn_w_up, l3_ffn_conv_w, l3_ffn_conv_b, l3_ffn_w_down, final_norm_g):
    rope = _rope_tables()
    xs = jnp.concatenate([ctx, x], axis=1)
    cpad = jnp.concatenate([c, c_ctx[None, :], jnp.zeros((MOD_ROWS - B - 1, D), F32)], axis=0)

    m = _mod(cpad, l0_ada_w, l0_ada_b)
    xs = _da_layer(0, xs, m, rope, l0_norm1_g, l0_da_w_in, l0_da_lam_q1, l0_da_lam_k1, l0_da_lam_q2, l0_da_lam_k2,
                   l0_da_subln_g, l0_da_w_out)
    xs = _ffn_layer(0, xs, m, l0_norm2_g, l0_ffn_w_up, l0_ffn_conv_w, l0_ffn_conv_b, l0_ffn_w_down, final_norm_g)

    m = _mod(cpad, l1_ada_w, l1_ada_b)
    xs = _ml_layer(xs, m, l1_norm1_g, l1_ml_w_in, l1_ml_gate_b, l1_ml_norm_g, l1_ml_w_out)
    xs = _ffn_layer(1, xs, m, l1_norm2_g, l1_ffn_w_up, l1_ffn_conv_w, l1_ffn_conv_b, l1_ffn_w_down, final_norm_g)

    m = _mod(cpad, l2_ada_w, l2_ada_b)
    xs = _sw_layer(xs, m, rope, l2_norm1_g, l2_sw_w_in, l2_sw_sink, l2_sw_w_out)
    xs = _ffn_layer(2, xs, m, l2_norm2_g, l2_ffn_w_up, l2_ffn_conv_w, l2_ffn_conv_b, l2_ffn_w_down, final_norm_g)

    m = _mod(cpad, l3_ada_w, l3_ada_b)
    xs = _da_layer(3, xs, m, rope, l3_norm1_g, l3_da_w_in, l3_da_lam_q1, l3_da_lam_k1, l3_da_lam_q2, l3_da_lam_k2,
                   l3_da_subln_g, l3_da_w_out)
    return _ffn_layer(3, xs, m, l3_norm2_g, l3_ffn_w_up, l3_ffn_conv_w, l3_ffn_conv_b, l3_ffn_w_down, final_norm_g)
```

```python
import functools
import math

import jax
import jax.numpy as jnp
from jax import lax
from jax.experimental import pallas as pl
from jax.experimental.pallas import tpu as pltpu

F32 = jnp.float32
BF16 = jnp.bfloat16
HIGHEST = lax.Precision.HIGHEST

D = 1024
B = 8
S = 2048
CTX = 256
T = CTX + S
DEPTH = 4
GRID_W = 64
HEAD_DIM = 64
EPS = 1e-6
NEG = -1e30

TM = 256
NT = T // TM
NS = S // TM
MOD_ROWS = 16

DA_HEADS = 8
DA_HPS = 2
ML_HEADS = 8
ML_V = 128
SW_KV = 4
SW_GRP = 4
SW_WIN = 128
SW_QB = 128
FFN = 2816
FFN_CHUNK = 256
HALO = 8

LOG2E = math.log2(math.e)
Q_SCALE = HEAD_DIM ** -0.5 * LOG2E

NT_DIMS = (((1,), (1,)), ((), ()))
TN_DIMS = (((0,), (0,)), ((), ()))

VMEM_LIMIT = 56 * 1024 * 1024


def _cparams(*sem):
    return pltpu.CompilerParams(dimension_semantics=sem, vmem_limit_bytes=VMEM_LIMIT)


def _resident(shape):
    zeros = (0,) * len(shape)
    return pl.BlockSpec(shape, lambda *_: zeros, pipeline_mode=pl.Buffered(1))


def _rms(x, g):
    ms = jnp.mean(x * x, axis=-1, keepdims=True)
    return x * lax.rsqrt(ms + EPS) * g


def _norm_mod(x, g, shift, scale):
    return _rms(x, g) * (1.0 + scale) + shift


def _mod_ix_ctx(b, i):
    return (jnp.where(i == 0, B, b), 0, 0)


def _mod_ix_lat(b, i):
    return (b, 0, 0)


def _ada_kernel(c_ref, w_ref, b_ref, o_ref):
    c = c_ref[...]
    a = (c * jax.nn.sigmoid(c)).astype(BF16)
    o_ref[...] = jnp.dot(a, w_ref[...].astype(BF16), preferred_element_type=F32) + b_ref[...]


def _ada(cpad, w, bias):
    n = w.shape[1]
    tn = 1024
    return pl.pallas_call(
        _ada_kernel,
        grid=(n // tn,),
        in_specs=[pl.BlockSpec((MOD_ROWS, D), lambda j: (0, 0)),
                  pl.BlockSpec((D, tn), lambda j: (0, j)),
                  pl.BlockSpec((1, tn), lambda j: (0, j))],
        out_specs=pl.BlockSpec((MOD_ROWS, tn), lambda j: (0, j)),
        out_shape=jax.ShapeDtypeStruct((MOD_ROWS, n), F32),
        compiler_params=_cparams("arbitrary"),
        name="ada",
    )(cpad, w, bias.reshape(1, n))


def _rope(c, cos, sa, sb):
    return c * cos + pltpu.roll(c, 112, 1) * sa + pltpu.roll(c, 16, 1) * sb


def _attn_proj_kernel(x_ref, mod_ref, g_ref, w_ref, cos_ref, sa_ref, sb_ref, o_ref, *, nq, nk, nv):
    h = _norm_mod(x_ref[...], g_ref[...], mod_ref[0:1, :], mod_ref[1:2, :]).astype(BF16)
    cos, sa, sb = cos_ref[...], sa_ref[...], sb_ref[...]
    for j in range(nq + nk + nv):
        y = jnp.dot(h, w_ref[:, j * 256:(j + 1) * 256], preferred_element_type=F32)
        if j < nq + nk:
            halves = []
            for t in range(2):
                r = _rope(y[:, t * 128:(t + 1) * 128], cos, sa, sb)
                halves.append(r * Q_SCALE if j < nq else r)
            y = jnp.concatenate(halves, axis=1)
        o_ref[:, j * 256:(j + 1) * 256] = y.astype(BF16)


def _attn_proj(xs, mod, g, w, rope, nq, nk, nv):
    n = (nq + nk + nv) * 256
    cos, sa, sb = rope
    tab = pl.BlockSpec((TM, 128), lambda b, i: (i, 0))
    return pl.pallas_call(
        functools.partial(_attn_proj_kernel, nq=nq, nk=nk, nv=nv),
        grid=(B, NT),
        in_specs=[pl.BlockSpec((None, TM, D), lambda b, i: (b, i, 0)),
                  pl.BlockSpec((None, 6, D), _mod_ix_ctx),
                  _resident((1, D)),
                  _resident((D, n)),
                  tab, tab, tab],
        out_specs=pl.BlockSpec((None, TM, n), lambda b, i: (b, i, 0)),
        out_shape=jax.ShapeDtypeStruct((B, T, n), BF16),
        compiler_params=_cparams("parallel", "parallel"),
        name="attn_proj",
    )(xs, mod, g, w, cos, sa, sb)


def _ml_proj_kernel(x_ref, mod_ref, g_ref, w_ref, wg_ref, wgt_ref, gb_ref, gbt_ref,
                    qk_ref, v_ref, o_ref, gc_ref, gr_ref):
    h = _norm_mod(x_ref[...], g_ref[...], mod_ref[0:1, :], mod_ref[1:2, :]).astype(BF16)
    for j in range(12):
        y = jnp.dot(h, w_ref[:, j * 256:(j + 1) * 256], preferred_element_type=F32)
        if j < 2:
            qk_ref[:, j * 256:(j + 1) * 256] = (y * 0.125).astype(BF16)
        elif j < 4:
            qk_ref[:, j * 256:(j + 1) * 256] = y.astype(BF16)
        elif j < 8:
            v_ref[:, (j - 4) * 256:(j - 3) * 256] = y.astype(BF16)
        else:
            o_ref[:, (j - 8) * 256:(j - 7) * 256] = y
    gc_ref[...] = jnp.dot(h, wg_ref[...], preferred_element_type=F32) + gb_ref[...]
    gr_ref[...] = lax.dot_general(wgt_ref[...], h, NT_DIMS, preferred_element_type=F32) + gbt_ref[...]


def _ml_proj(xs, mod, g, w, wg, wgt, gb, gbt):
    tile = lambda n, dt: (pl.BlockSpec((None, TM, n), lambda b, i: (b, i, 0)),
                          jax.ShapeDtypeStruct((B, T, n), dt))
    outs = [tile(D, BF16), tile(D, BF16), tile(D, F32), tile(32, F32),
            (pl.BlockSpec((None, 32, TM), lambda b, i: (b, 0, i)), jax.ShapeDtypeStruct((B, 32, T), F32))]
    return pl.pallas_call(
        _ml_proj_kernel,
        grid=(B, NT),
        in_specs=[pl.BlockSpec((None, TM, D), lambda b, i: (b, i, 0)),
                  pl.BlockSpec((None, 6, D), _mod_ix_ctx),
                  _resident((1, D)),
                  _resident((D, 3072)),
                  _resident((D, 32)),
                  _resident((32, D)),
                  _resident((1, 32)),
                  _resident((32, 1))],
        out_specs=[o[0] for o in outs],
        out_shape=[o[1] for o in outs],
        compiler_params=_cparams("parallel", "parallel"),
        name="ml_proj",
    )(xs, mod, g, w, wg, wgt, gb, gbt)


def _da_attn_kernel(lam_ref, g_ref, q_ref, k_ref, v_ref, o_ref, *, lam_init, has_ctx_tile):
    i = pl.program_id(2)
    lv = lam_ref[...]
    lam = (jnp.exp(jnp.sum(lv[0:1] * lv[1:2], axis=-1, keepdims=True))
           - jnp.exp(jnp.sum(lv[2:3] * lv[3:4], axis=-1, keepdims=True)) + lam_init)
    lane = lax.broadcasted_iota(jnp.int32, (TM, 128), 1)

    def attend(nk):
        for hh in range(DA_HPS):
            sl = slice(hh * 128, (hh + 1) * 128)
            q = q_ref[:, sl]
            k = k_ref[0:nk, sl]
            v = v_ref[0:nk, sl]
            es, ls = [], []
            for c in range(2):
                own = lane < HEAD_DIM if c == 0 else lane >= HEAD_DIM
                s = lax.dot_general(jnp.where(own, q, jnp.zeros_like(q)), k, NT_DIMS, preferred_element_type=F32)
                e = jnp.exp2(s - jnp.max(s, axis=-1, keepdims=True))
                es.append(e)
                ls.append(jnp.sum(e, axis=-1, keepdims=True))
            w = es[0] - es[1] * (lam * ls[0] / ls[1])
            o = jnp.dot(w.astype(BF16), v, preferred_element_type=F32) * (1.0 / ls[0])
            o_ref[:, sl] = (_rms(o, g_ref[...]) * (1.0 - lam_init)).astype(BF16)

    if has_ctx_tile:
        pl.when(i == 0)(lambda: attend(CTX))
        pl.when(i > 0)(lambda: attend(T))
    else:
        attend(T)


def _da_attn(qkv, lamv, subln_g, lam_init, need_ctx):
    off = 0 if need_ctx else 1
    nq = NT - off
    ng = DA_HEADS // DA_HPS
    wl = DA_HPS * 128
    return pl.pallas_call(
        functools.partial(_da_attn_kernel, lam_init=lam_init, has_ctx_tile=need_ctx),
        grid=(B, ng, nq),
        in_specs=[_resident((4, HEAD_DIM)),
                  _resident((1, 128)),
                  pl.BlockSpec((None, TM, wl), lambda b, h, i: (b, i + off, h)),
                  pl.BlockSpec((None, T, wl), lambda b, h, i: (b, 0, ng + h)),
                  pl.BlockSpec((None, T, wl), lambda b, h, i: (b, 0, 2 * ng + h))],
        out_specs=pl.BlockSpec((None, TM, wl), lambda b, h, i: (b, i, h)),
        out_shape=jax.ShapeDtypeStruct((B, nq * TM, D), BF16),
        compiler_params=_cparams("parallel", "parallel", "parallel"),
        name="da_attn",
    )(lamv, subln_g, qkv, qkv, qkv)


def _swa_kernel(sink_ref, q_ref, k_ref, v_ref, o_ref):
    blk = pl.program_id(1)
    nwin = 3 * SW_QB
    rows = SW_GRP * SW_QB
    lane_q = lax.broadcasted_iota(jnp.int32, (rows, 256), 1) // HEAD_DIM
    grp = lax.broadcasted_iota(jnp.int32, (rows, 1), 0) // SW_QB

    def body(is_lat):
        qs = jnp.concatenate([q_ref[:, g * 256:(g + 1) * 256] for g in range(SW_GRP)], axis=0)
        kc = k_ref[0:CTX, :]
        vc = v_ref[0:CTX, :]
        lane_c = lax.broadcasted_iota(jnp.int32, (CTX, 256), 1) // HEAD_DIM
        if is_lat:
            j = blk - CTX // SW_QB
            start = pl.multiple_of(jnp.minimum(CTX + (j - 1) * SW_QB, T - nwin), SW_QB)
            kw = k_ref[pl.ds(start, nwin), :]
            vw = v_ref[pl.ds(start, nwin), :]
            lane_w = lax.broadcasted_iota(jnp.int32, (nwin, 256), 1) // HEAD_DIM
            qpos = j * SW_QB + lax.broadcasted_iota(jnp.int32, (rows, nwin), 0) % SW_QB
            kpos = start - CTX + lax.broadcasted_iota(jnp.int32, (rows, nwin), 1)
            band = (jnp.abs(qpos - kpos) <= SW_WIN) & (kpos >= 0)
        acc = jnp.zeros((rows, 256), F32)
        for h in range(SW_KV):
            sink = jnp.zeros((rows, 1), F32)
            for g in range(SW_GRP):
                sg = sink_ref[0:1, SW_GRP * h + g:SW_GRP * h + g + 1] * LOG2E
                sink = jnp.where(grp == g, sg, sink)
            kcz = jnp.where(lane_c == h, kc, jnp.zeros_like(kc))
            s_c = lax.dot_general(qs, kcz, NT_DIMS, preferred_element_type=F32)
            m = jnp.maximum(jnp.max(s_c, axis=-1, keepdims=True), sink)
            if is_lat:
                kwz = jnp.where(lane_w == h, kw, jnp.zeros_like(kw))
                s_w = lax.dot_general(qs, kwz, NT_DIMS, preferred_element_type=F32)
                s_w = jnp.where(band, s_w, NEG)
                m = jnp.maximum(m, jnp.max(s_w, axis=-1, keepdims=True))
            e_c = jnp.exp2(s_c - m)
            den = jnp.sum(e_c, axis=-1, keepdims=True) + jnp.exp2(sink - m)
            if is_lat:
                e_w = jnp.exp2(s_w - m)
                den = den + jnp.sum(e_w, axis=-1, keepdims=True)
            r = 1.0 / den
            o = jnp.dot((e_c * r).astype(BF16), vc, preferred_element_type=F32)
            if is_lat:
                o = o + jnp.dot((e_w * r).astype(BF16), vw, preferred_element_type=F32)
            acc = jnp.where(lane_q == h, o, acc)
        for g in range(SW_GRP):
            o_ref[:, g * 256:(g + 1) * 256] = acc[g * SW_QB:(g + 1) * SW_QB, :].astype(BF16)

    pl.when(blk < CTX // SW_QB)(lambda: body(False))
    pl.when(blk >= CTX // SW_QB)(lambda: body(True))


def _swa_attn(qkv, sink):
    return pl.pallas_call(
        _swa_kernel,
        grid=(B, T // SW_QB),
        in_specs=[_resident((1, 16)),
                  pl.BlockSpec((None, SW_QB, D), lambda b, i: (b, i, 0)),
                  pl.BlockSpec((None, T, 256), lambda b, i: (b, 0, 4)),
                  pl.BlockSpec((None, T, 256), lambda b, i: (b, 0, 5))],
        out_specs=pl.BlockSpec((None, SW_QB, D), lambda b, i: (b, i, 0)),
        out_shape=jax.ShapeDtypeStruct((B, T, D), BF16),
        compiler_params=_cparams("parallel", "parallel"),
        name="swa_attn",
    )(sink, qkv, qkv, qkv)


def _logsig(x):
    return jnp.minimum(x, 0.0) - jnp.log(1.0 + jnp.exp(-jnp.abs(x)))


def _mlstm_kernel(qkf_ref, vf_ref, gcf_ref, grf_ref, qkb_ref, vb_ref, gcb_ref, grb_ref,
                  hf_ref, hb_ref, st_ref, m_ref):
    step = pl.program_id(1)

    @pl.when(step == 0)
    def _():
        st_ref[...] = jnp.zeros_like(st_ref)
        m_ref[...] = jnp.zeros_like(m_ref)

    L = TM
    row = lax.broadcasted_iota(jnp.int32, (L, L), 0)
    col = lax.broadcasted_iota(jnp.int32, (L, L), 1)
    lower = col <= row
    upper = col >= row
    lower_f = lower.astype(F32)
    upper_f = upper.astype(F32)
    lane = lax.broadcasted_iota(jnp.int32, (L, 128), 1)
    ones_blk = jnp.where(lane == 0, 1.0, 0.0).astype(BF16)

    dirs = ((qkf_ref, vf_ref, gcf_ref, grf_ref, hf_ref), (qkb_ref, vb_ref, gcb_ref, grb_ref, hb_ref))
    for d, (qk_ref, v_ref, gc_ref, gr_ref, h_ref) in enumerate(dirs):
        tri = lower if d == 0 else upper
        tri_f, tri_t_f = (lower_f, upper_f) if d == 0 else (upper_f, lower_f)
        last = L - 1 if d == 0 else 0
        gc = gc_ref[...]
        gr = gr_ref[...]
        lf_c = _logsig(gc[:, 16 * d + 8:16 * d + 16])
        lf_r = _logsig(gr[16 * d + 8:16 * d + 16, :])
        cum_c = jnp.dot(tri_f, lf_c, precision=HIGHEST, preferred_element_type=F32)
        cum_r = jnp.dot(lf_r, tri_t_f, precision=HIGHEST, preferred_element_type=F32)
        for h in range(ML_HEADS):
            idx = d * ML_HEADS + h
            p = h // 2
            q2 = qk_ref[:, p * 128:(p + 1) * 128]
            k2 = qk_ref[:, 512 + p * 128:512 + (p + 1) * 128]
            own = lane >= HEAD_DIM if h % 2 else lane < HEAD_DIM
            qh = jnp.where(own, q2, jnp.zeros_like(q2))
            vh = v_ref[:, h * ML_V:(h + 1) * ML_V]
            vaug = jnp.concatenate([vh, ones_blk], axis=1)
            b_c = cum_c[:, h:h + 1]
            b_r = cum_r[h:h + 1, :]
            i_c = gc[:, 16 * d + h:16 * d + h + 1]
            i_r = gr[16 * d + h:16 * d + h + 1, :]
            m_prev = m_ref[idx][0:1, 0:1]
            st = st_ref[idx]

            dm = jnp.where(tri, b_c - b_r + i_r, NEG)
            a_c = b_c + m_prev
            m_t = jnp.maximum(a_c, jnp.max(dm, axis=-1, keepdims=True))
            inter = jnp.exp(a_c - m_t)
            sc = lax.dot_general(qh, k2, NT_DIMS, preferred_element_type=F32) * jnp.exp(dm - m_t)
            carry = jnp.dot(qh, st.astype(BF16), preferred_element_type=F32)
            num = inter * carry[:, 0:ML_V] + jnp.dot(sc.astype(BF16), vh, preferred_element_type=F32)
            den = inter * carry[:, ML_V:ML_V + 1] + jnp.sum(sc, axis=-1, keepdims=True)
            h_ref[:, h * ML_V:(h + 1) * ML_V] = num / jnp.maximum(jnp.abs(den), jnp.exp(-m_t))

            m_new = m_t[last:last + 1, :]
            b_last = b_c[last:last + 1, :]
            g_c = jnp.exp(b_last - b_c + i_c - m_new)
            decay = jnp.exp(b_last + m_prev - m_new)
            gv = (g_c * vaug.astype(F32)).astype(BF16)
            st_ref[idx] = decay * st + lax.dot_general(k2, gv, TN_DIMS, preferred_element_type=F32)
            m_ref[idx] = jnp.broadcast_to(m_new, (8, 128))


def _mlstm(qk, v, gc, gr):
    fwd = lambda b, s: (b, s, 0)
    bwd = lambda b, s: (b, jnp.where(s == 0, 0, NT - s), 0)
    fwd_t = lambda b, s: (b, 0, s)
    bwd_t = lambda b, s: (b, 0, jnp.where(s == 0, 0, NT - s))
    tok = lambda n, ix: pl.BlockSpec((None, TM, n), ix)
    return pl.pallas_call(
        _mlstm_kernel,
        grid=(B, NT),
        in_specs=[tok(D, fwd), tok(D, fwd), tok(32, fwd), pl.BlockSpec((None, 32, TM), fwd_t),
                  tok(D, bwd), tok(D, bwd), tok(32, bwd), pl.BlockSpec((None, 32, TM), bwd_t)],
        out_specs=[tok(D, fwd), tok(D, bwd)],
        out_shape=[jax.ShapeDtypeStruct((B, T, D), F32)] * 2,
        scratch_shapes=[pltpu.VMEM((2 * ML_HEADS, 128, 256), F32),
                        pltpu.VMEM((2 * ML_HEADS, 8, 128), F32)],
        compiler_params=_cparams("parallel", "arbitrary"),
        name="mlstm",
    )(qk, v, gc, gr, qk, v, gc, gr)


def _out_proj_kernel(a_ref, w_ref, x_ref, mod_ref, o_ref):
    y = jnp.dot(a_ref[...], w_ref[...], preferred_element_type=F32)
    o_ref[...] = x_ref[...] + mod_ref[2:3, :] * y


def _out_proj(a, w, xs, mod, need_ctx):
    off = 0 if need_ctx else 1
    n = NT - off
    return pl.pallas_call(
        _out_proj_kernel,
        grid=(B, n),
        in_specs=[pl.BlockSpec((None, TM, D), lambda b, i: (b, i, 0)),
                  _resident((D, D)),
                  pl.BlockSpec((None, TM, D), lambda b, i: (b, i + off, 0)),
                  pl.BlockSpec((None, 6, D), _mod_ix_ctx if need_ctx else _mod_ix_lat)],
        out_specs=pl.BlockSpec((None, TM, D), lambda b, i: (b, i, 0)),
        out_shape=jax.ShapeDtypeStruct((B, n * TM, D), F32),
        compiler_params=_cparams("parallel", "parallel"),
        name="out_proj",
    )(a, w, xs, mod)


def _ml_out_kernel(hf_ref, hb_ref, og_ref, ng_ref, w_ref, x_ref, mod_ref, o_ref):
    parts = []
    for h in range(ML_HEADS):
        sl = slice(h * ML_V, (h + 1) * ML_V)
        y = _rms(hf_ref[:, sl] + hb_ref[:, sl], ng_ref[:, sl])
        parts.append((y * jax.nn.sigmoid(og_ref[:, sl])).astype(BF16))
    y = jnp.dot(jnp.concatenate(parts, axis=1), w_ref[...], preferred_element_type=F32)
    o_ref[...] = x_ref[...] + mod_ref[2:3, :] * y


def _ml_out(hf, hb, og, ng, w, xs, mod):
    tile = pl.BlockSpec((None, TM, D), lambda b, i: (b, i, 0))
    return pl.pallas_call(
        _ml_out_kernel,
        grid=(B, NT),
        in_specs=[tile, tile, tile, _resident((1, D)), _resident((D, D)), tile,
                  pl.BlockSpec((None, 6, D), _mod_ix_ctx)],
        out_specs=tile,
        out_shape=jax.ShapeDtypeStruct((B, T, D), F32),
        compiler_params=_cparams("parallel", "parallel"),
        name="ml_out",
    )(hf, hb, og, ng, w, xs, mod)


def _ffn_kernel(x_ref, xp_ref, xn_ref, mod_ref, g_ref, wup_ref, cw_ref, cb_ref, wdn_ref, fg_ref,
                o_ref, h_scr, ua_scr, ug_scr, act_scr, *, has_ctx, nt, final):
    i = pl.program_id(1)
    g = g_ref[...]
    shift, scale, gate = mod_ref[3:4, :], mod_ref[4:5, :], mod_ref[5:6, :]
    if has_ctx:
        prev_ok = i >= 2
        next_ok = (i != 0) & (i != nt - 1)
    else:
        prev_ok = i >= 1
        next_ok = i != nt - 1
    x = x_ref[...]
    h_scr[0:HALO, :] = jnp.where(prev_ok, _norm_mod(xp_ref[...], g, shift, scale), 0.0)
    h_scr[HALO:HALO + TM, :] = _norm_mod(x, g, shift, scale)
    h_scr[HALO + TM:, :] = jnp.where(next_ok, _norm_mod(xn_ref[...], g, shift, scale), 0.0)
    hb = h_scr[...].astype(BF16)

    def conv(scr, off):
        w = cw_ref[:, off:off + FFN_CHUNK]
        return (scr[HALO - 1:HALO - 1 + TM, :] * w[0:1] + scr[HALO:HALO + TM, :] * w[1:2]
                + scr[HALO + 1:HALO + 1 + TM, :] * w[2:3] + cb_ref[:, off:off + FFN_CHUNK])

    for c in range(FFN // FFN_CHUNK):
        lo = c * FFN_CHUNK
        ua_scr[c % 2] = jnp.dot(hb, wup_ref[:, lo:lo + FFN_CHUNK], preferred_element_type=F32)
        ug_scr[c % 2] = jnp.dot(hb, wup_ref[:, FFN + lo:FFN + lo + FFN_CHUNK], preferred_element_type=F32)
        a = conv(ua_scr.at[c % 2], lo)
        gg = conv(ug_scr.at[c % 2], FFN + lo)
        act_scr[:, lo:lo + FFN_CHUNK] = (a * (gg * jax.nn.sigmoid(gg))).astype(BF16)
    y = jnp.dot(act_scr[...], wdn_ref[...], preferred_element_type=F32)
    out = x + gate * y
    if final:
        out = _rms(out, fg_ref[...])
    o_ref[...] = out


def _ffn(xs, mod, g, wup, cw, cb, wdn, fg, has_ctx, final):
    rows = xs.shape[1]
    nt = rows // TM
    per_tile = TM // HALO
    last_halo = rows // HALO - 1
    return pl.pallas_call(
        functools.partial(_ffn_kernel, has_ctx=has_ctx, nt=nt, final=final),
        grid=(B, nt),
        in_specs=[pl.BlockSpec((None, TM, D), lambda b, i: (b, i, 0)),
                  pl.BlockSpec((None, HALO, D), lambda b, i: (b, jnp.maximum(i * per_tile - 1, 0), 0)),
                  pl.BlockSpec((None, HALO, D), lambda b, i: (b, jnp.minimum((i + 1) * per_tile, last_halo), 0)),
                  pl.BlockSpec((None, 6, D), _mod_ix_ctx if has_ctx else _mod_ix_lat),
                  _resident((1, D)),
                  _resident((D, 2 * FFN)),
                  _resident((3, 2 * FFN)),
                  _resident((1, 2 * FFN)),
                  _resident((FFN, D)),
                  _resident((1, D))],
        out_specs=pl.BlockSpec((None, TM, D), lambda b, i: (b, i, 0)),
        out_shape=jax.ShapeDtypeStruct((B, rows, D), F32),
        scratch_shapes=[pltpu.VMEM((TM + 2 * HALO, D), F32),
                        pltpu.VMEM((2, TM + 2 * HALO, FFN_CHUNK), F32),
                        pltpu.VMEM((2, TM + 2 * HALO, FFN_CHUNK), F32),
                        pltpu.VMEM((TM, FFN), BF16)],
        compiler_params=_cparams("parallel", "parallel"),
        name="ffn",
    )(xs, xs, xs, mod, g, wup, cw, cb, wdn, fg)


def _rope_tables():
    rows = S // GRID_W
    quarter = HEAD_DIM // 4
    row = jnp.repeat(jnp.arange(rows, dtype=F32), GRID_W)
    col = jnp.tile(jnp.arange(GRID_W, dtype=F32), rows)
    inv = 10000.0 ** (-jnp.arange(quarter, dtype=F32) / quarter)
    ang_r = row[:, None] * inv
    ang_c = col[:, None] * inv
    cr, sr, cc, sc = jnp.cos(ang_r), jnp.sin(ang_r), jnp.cos(ang_c), jnp.sin(ang_c)
    z = jnp.zeros_like(sr)
    cos = jnp.concatenate([cr, cr, cc, cc], axis=1)
    sa = jnp.concatenate([-sr, z, -sc, z], axis=1)
    sb = jnp.concatenate([z, sr, z, sc], axis=1)
    pad = lambda t, v: jnp.concatenate([jnp.full((CTX, 128), v, F32), jnp.tile(t, (1, 2))], axis=0)
    return pad(cos, 1.0), pad(sa, 0.0), pad(sb, 0.0)


def _group_major(w, axis):
    shape = w.shape
    w = w.reshape(shape[:axis] + (SW_KV, SW_GRP, HEAD_DIM) + shape[axis + 1:])
    w = jnp.swapaxes(w, axis, axis + 1)
    return w.reshape(shape)


def _row(v):
    return v.reshape(1, -1)


def _bf(w):
    return w.astype(BF16)


def _mod(cpad, w, b):
    return _ada(cpad, w, b).reshape(MOD_ROWS, 6, D)


def _da_layer(idx, xs, mod, rope, norm1_g, w_in, lq1, lk1, lq2, lk2, subln_g, w_out):
    need_ctx = idx < DEPTH - 1
    lam_init = 0.8 - 0.6 * math.exp(-0.3 * idx)
    qkv = _attn_proj(xs, mod, _row(norm1_g), _bf(w_in), rope, 4, 4, 4)
    o = _da_attn(qkv, jnp.stack([lq1, lk1, lq2, lk2]), _row(subln_g), lam_init, need_ctx)
    return _out_proj(o, _bf(w_out), xs, mod, need_ctx)


def _ml_layer(xs, mod, norm1_g, w_in, gate_b, norm_g, w_out):
    wg = _bf(w_in[:, 3072:])
    qk, v, og, gc, gr = _ml_proj(xs, mod, _row(norm1_g), _bf(w_in[:, :3072]), wg, wg.T,
                                 _row(gate_b), gate_b.reshape(-1, 1))
    hf, hb = _mlstm(qk, v, gc, gr)
    return _ml_out(hf, hb, og, _row(norm_g), _bf(w_out), xs, mod)


def _sw_layer(xs, mod, rope, norm1_g, w_in, sink, w_out):
    w_q = _group_major(w_in[:, :D], 1)
    qkv = _attn_proj(xs, mod, _row(norm1_g), _bf(jnp.concatenate([w_q, w_in[:, D:]], axis=1)), rope, 4, 1, 1)
    o = _swa_attn(qkv, _row(sink))
    return _out_proj(o, _bf(_group_major(w_out, 0)), xs, mod, True)


def _ffn_layer(idx, xs, mod, norm2_g, w_up, conv_w, conv_b, w_down, final_norm_g):
    final = idx == DEPTH - 1
    return _ffn(xs, mod, _row(norm2_g), _bf(w_up), conv_w, _row(conv_b), _bf(w_down),
                _row(final_norm_g), has_ctx=not final, final=final)


def kernel(x, c, ctx, c_ctx, l0_ada_w, l0_ada_b, l0_norm1_g, l0_da_w_in, l0_da_lam_q1, l0_da_lam_k1, l0_da_lam_q2, l0_da_lam_k2, l0_da_subln_g, l0_da_w_out, l0_norm2_g, l0_ffn_w_up, l0_ffn_conv_w, l0_ffn_conv_b, l0_ffn_w_down, l1_ada_w, l1_ada_b, l1_norm1_g, l1_ml_w_in, l1_ml_gate_b, l1_ml_norm_g, l1_ml_w_out, l1_norm2_g, l1_ffn_w_up, l1_ffn_conv_w, l1_ffn_conv_b, l1_ffn_w_down, l2_ada_w, l2_ada_b, l2_norm1_g, l2_sw_w_in, l2_sw_sink, l2_sw_w_out, l2_norm2_g, l2_ffn_w_up, l2_ffn_conv_w, l2_ffn_conv_b, l2_ffn_w_down, l3_ada_w, l3_ada_b, l3_norm1_g, l3_da_w_in, l3_da_lam_q1, l3_da_lam_k1, l3_da_lam_q2, l3_da_lam_k2, l3_da_subln_g, l3_da_w_out, l3_norm2_g, l3_ffn_w_up, l3_ffn_conv_w, l3_ffn_conv_b, l3_ffn_w_down, final_norm_g):
    rope = _rope_tables()
    xs = jnp.concatenate([ctx, x], axis=1)
    cpad = jnp.concatenate([c, c_ctx[None, :], jnp.zeros((MOD_ROWS - B - 1, D), F32)], axis=0)

    m = _mod(cpad, l0_ada_w, l0_ada_b)
    xs = _da_layer(0, xs, m, rope, l0_norm1_g, l0_da_w_in, l0_da_lam_q1, l0_da_lam_k1, l0_da_lam_q2, l0_da_lam_k2,
                   l0_da_subln_g, l0_da_w_out)
    xs = _ffn_layer(0, xs, m, l0_norm2_g, l0_ffn_w_up, l0_ffn_conv_w, l0_ffn_conv_b, l0_ffn_w_down, final_norm_g)

    m = _mod(cpad, l1_ada_w, l1_ada_b)
    xs = _ml_layer(xs, m, l1_norm1_g, l1_ml_w_in, l1_ml_gate_b, l1_ml_norm_g, l1_ml_w_out)
    xs = _ffn_layer(1, xs, m, l1_norm2_g, l1_ffn_w_up, l1_ffn_conv_w, l1_ffn_conv_b, l1_ffn_w_down, final_norm_g)

    m = _mod(cpad, l2_ada_w, l2_ada_b)
    xs = _sw_layer(xs, m, rope, l2_norm1_g, l2_sw_w_in, l2_sw_sink, l2_sw_w_out)
    xs = _ffn_layer(2, xs, m, l2_norm2_g, l2_ffn_w_up, l2_ffn_conv_w, l2_ffn_conv_b, l2_ffn_w_down, final_norm_g)

    m = _mod(cpad, l3_ada_w, l3_ada_b)
    xs = _da_layer(3, xs, m, rope, l3_norm1_g, l3_da_w_in, l3_da_lam_q1, l3_da_lam_k1, l3_da_lam_q2, l3_da_lam_k2,
                   l3_da_subln_g, l3_da_w_out)
    return _ffn_layer(3, xs, m, l3_norm2_g, l3_ffn_w_up, l3_ffn_conv_w, l3_ffn_conv_b, l3_ffn_w_down, final_norm_g)
```

```python
import functools
import math

import jax
import jax.numpy as jnp
from jax import lax
from jax.experimental import pallas as pl
from jax.experimental.pallas import tpu as pltpu

F32 = jnp.float32
BF16 = jnp.bfloat16
HIGHEST = lax.Precision.HIGHEST

D = 1024
B = 8
S = 2048
CTX = 256
T = CTX + S
DEPTH = 4
GRID_W = 64
HEAD_DIM = 64
EPS = 1e-6
NEG = -1e30

TM = 256
NT = T // TM
NS = S // TM
MOD_ROWS = 16

DA_HEADS = 8
DA_HPS = 4
ML_HEADS = 8
ML_V = 128
SW_KV = 4
SW_GRP = 4
SW_WIN = 128
SW_QB = 128
FFN = 2816
FFN_CHUNK = 256
HALO = 8

LOG2E = math.log2(math.e)
Q_SCALE = HEAD_DIM ** -0.5 * LOG2E

NT_DIMS = (((1,), (1,)), ((), ()))
TN_DIMS = (((0,), (0,)), ((), ()))

VMEM_LIMIT = 56 * 1024 * 1024


def _cparams(*sem):
    return pltpu.CompilerParams(dimension_semantics=sem, vmem_limit_bytes=VMEM_LIMIT)


def _resident(shape):
    zeros = (0,) * len(shape)
    return pl.BlockSpec(shape, lambda *_: zeros, pipeline_mode=pl.Buffered(1))


def _rms(x, g):
    ms = jnp.mean(x * x, axis=-1, keepdims=True)
    return x * lax.rsqrt(ms + EPS) * g


def _norm_mod(x, g, shift, scale):
    return _rms(x, g) * (1.0 + scale) + shift


def _mod_ix_ctx(b, i):
    return (jnp.where(i == 0, B, b), 0, 0)


def _mod_ix_lat(b, i):
    return (b, 0, 0)


def _ada_kernel(c_ref, w_ref, b_ref, o_ref):
    c = c_ref[...]
    a = (c * jax.nn.sigmoid(c)).astype(BF16)
    o_ref[...] = jnp.dot(a, w_ref[...].astype(BF16), preferred_element_type=F32) + b_ref[...]


def _ada(cpad, w, bias):
    n = w.shape[1]
    tn = 1024
    return pl.pallas_call(
        _ada_kernel,
        grid=(n // tn,),
        in_specs=[pl.BlockSpec((MOD_ROWS, D), lambda j: (0, 0)),
                  pl.BlockSpec((D, tn), lambda j: (0, j)),
                  pl.BlockSpec((1, tn), lambda j: (0, j))],
        out_specs=pl.BlockSpec((MOD_ROWS, tn), lambda j: (0, j)),
        out_shape=jax.ShapeDtypeStruct((MOD_ROWS, n), F32),
        compiler_params=_cparams("arbitrary"),
        name="ada",
    )(cpad, w, bias.reshape(1, n))


def _rope(c, cos, sa, sb):
    return c * cos + pltpu.roll(c, 112, 1) * sa + pltpu.roll(c, 16, 1) * sb


def _attn_proj_kernel(x_ref, mod_ref, g_ref, w_ref, cos_ref, sa_ref, sb_ref, o_ref, *, nq, nk, nv):
    h = _norm_mod(x_ref[...], g_ref[...], mod_ref[0:1, :], mod_ref[1:2, :]).astype(BF16)
    cos, sa, sb = cos_ref[...], sa_ref[...], sb_ref[...]
    for j in range(nq + nk + nv):
        y = jnp.dot(h, w_ref[:, j * 256:(j + 1) * 256], preferred_element_type=F32)
        if j < nq + nk:
            halves = []
            for t in range(2):
                r = _rope(y[:, t * 128:(t + 1) * 128], cos, sa, sb)
                halves.append(r * Q_SCALE if j < nq else r)
            y = jnp.concatenate(halves, axis=1)
        o_ref[:, j * 256:(j + 1) * 256] = y.astype(BF16)


def _attn_proj(xs, mod, g, w, rope, nq, nk, nv):
    n = (nq + nk + nv) * 256
    cos, sa, sb = rope
    tab = pl.BlockSpec((TM, 128), lambda b, i: (i, 0))
    return pl.pallas_call(
        functools.partial(_attn_proj_kernel, nq=nq, nk=nk, nv=nv),
        grid=(B, NT),
        in_specs=[pl.BlockSpec((None, TM, D), lambda b, i: (b, i, 0)),
                  pl.BlockSpec((None, 6, D), _mod_ix_ctx),
                  _resident((1, D)),
                  _resident((D, n)),
                  tab, tab, tab],
        out_specs=pl.BlockSpec((None, TM, n), lambda b, i: (b, i, 0)),
        out_shape=jax.ShapeDtypeStruct((B, T, n), BF16),
        compiler_params=_cparams("parallel", "parallel"),
        name="attn_proj",
    )(xs, mod, g, w, cos, sa, sb)


def _ml_proj_kernel(x_ref, mod_ref, g_ref, w_ref, wg_ref, wgt_ref, gb_ref, gbt_ref,
                    qk_ref, v_ref, o_ref, gc_ref, gr_ref):
    h = _norm_mod(x_ref[...], g_ref[...], mod_ref[0:1, :], mod_ref[1:2, :]).astype(BF16)
    for j in range(12):
        y = jnp.dot(h, w_ref[:, j * 256:(j + 1) * 256], preferred_element_type=F32)
        if j < 2:
            qk_ref[:, j * 256:(j + 1) * 256] = (y * 0.125).astype(BF16)
        elif j < 4:
            qk_ref[:, j * 256:(j + 1) * 256] = y.astype(BF16)
        elif j < 8:
            v_ref[:, (j - 4) * 256:(j - 3) * 256] = y.astype(BF16)
        else:
            o_ref[:, (j - 8) * 256:(j - 7) * 256] = y
    gc_ref[...] = jnp.dot(h, wg_ref[...], preferred_element_type=F32) + gb_ref[...]
    gr_ref[...] = lax.dot_general(wgt_ref[...], h, NT_DIMS, preferred_element_type=F32) + gbt_ref[...]


def _ml_proj(xs, mod, g, w, wg, wgt, gb, gbt):
    tile = lambda n, dt: (pl.BlockSpec((None, TM, n), lambda b, i: (b, i, 0)),
                          jax.ShapeDtypeStruct((B, T, n), dt))
    outs = [tile(D, BF16), tile(D, BF16), tile(D, F32), tile(32, F32),
            (pl.BlockSpec((None, 32, TM), lambda b, i: (b, 0, i)), jax.ShapeDtypeStruct((B, 32, T), F32))]
    return pl.pallas_call(
        _ml_proj_kernel,
        grid=(B, NT),
        in_specs=[pl.BlockSpec((None, TM, D), lambda b, i: (b, i, 0)),
                  pl.BlockSpec((None, 6, D), _mod_ix_ctx),
                  _resident((1, D)),
                  _resident((D, 3072)),
                  _resident((D, 32)),
                  _resident((32, D)),
                  _resident((1, 32)),
                  _resident((32, 1))],
        out_specs=[o[0] for o in outs],
        out_shape=[o[1] for o in outs],
        compiler_params=_cparams("parallel", "parallel"),
        name="ml_proj",
    )(xs, mod, g, w, wg, wgt, gb, gbt)


def _da_attn_kernel(lam_ref, g_ref, q_ref, k_ref, v_ref, o_ref, s_scr, *, lam_init, has_ctx_tile):
    i = pl.program_id(2)
    lv = lam_ref[...]
    lam = (jnp.exp(jnp.sum(lv[0:1] * lv[1:2], axis=-1, keepdims=True))
           - jnp.exp(jnp.sum(lv[2:3] * lv[3:4], axis=-1, keepdims=True)) + lam_init)
    lane = lax.broadcasted_iota(jnp.int32, (TM, 128), 1)

    def attend(nk):
        def scores(hh):
            sl = slice(hh * 128, (hh + 1) * 128)
            q = q_ref[:, sl]
            for c in range(2):
                own = lane < HEAD_DIM if c == 0 else lane >= HEAD_DIM
                s_scr[2 * hh + c, 0:nk, :] = lax.dot_general(
                    k_ref[0:nk, sl], jnp.where(own, q, jnp.zeros_like(q)), NT_DIMS, preferred_element_type=F32)

        scores(0)
        for hh in range(DA_HPS):
            if hh + 1 < DA_HPS:
                scores(hh + 1)
            sl = slice(hh * 128, (hh + 1) * 128)
            es, ls = [], []
            for c in range(2):
                s = s_scr[2 * hh + c, 0:nk, :]
                e = jnp.exp2(s - jnp.max(s, axis=0, keepdims=True))
                es.append(e)
                ls.append(jnp.sum(e, axis=0, keepdims=True))
            wt = es[0] - es[1] * (lam * ls[0] / ls[1])
            ot = lax.dot_general(v_ref[0:nk, sl], wt.astype(BF16), TN_DIMS, preferred_element_type=F32)
            o = (ot * (1.0 / ls[0])).T
            o_ref[:, sl] = (_rms(o, g_ref[...]) * (1.0 - lam_init)).astype(BF16)

    if has_ctx_tile:
        pl.when(i == 0)(lambda: attend(CTX))
        pl.when(i > 0)(lambda: attend(T))
    else:
        attend(T)


def _da_attn(qkv, lamv, subln_g, lam_init, need_ctx):
    off = 0 if need_ctx else 1
    nq = NT - off
    ng = DA_HEADS // DA_HPS
    wl = DA_HPS * 128
    return pl.pallas_call(
        functools.partial(_da_attn_kernel, lam_init=lam_init, has_ctx_tile=need_ctx),
        grid=(B, ng, nq),
        in_specs=[_resident((4, HEAD_DIM)),
                  _resident((1, 128)),
                  pl.BlockSpec((None, TM, wl), lambda b, h, i: (b, i + off, h)),
                  pl.BlockSpec((None, T, wl), lambda b, h, i: (b, 0, ng + h)),
                  pl.BlockSpec((None, T, wl), lambda b, h, i: (b, 0, 2 * ng + h))],
        out_specs=pl.BlockSpec((None, TM, wl), lambda b, h, i: (b, i, h)),
        out_shape=jax.ShapeDtypeStruct((B, nq * TM, D), BF16),
        scratch_shapes=[pltpu.VMEM((2 * DA_HPS, T, TM), F32)],
        compiler_params=_cparams("parallel", "parallel", "parallel"),
        name="da_attn",
    )(lamv, subln_g, qkv, qkv, qkv)


def _swa_kernel(sink_ref, q_ref, k_ref, v_ref, o_ref, s_scr, acc_scr):
    blk = pl.program_id(1)
    nwin = 3 * SW_QB
    nq = SW_GRP * SW_QB
    grp = lax.broadcasted_iota(jnp.int32, (1, nq), 1) // SW_QB

    def body(is_lat):
        qs = jnp.concatenate([q_ref[:, g * 256:(g + 1) * 256] for g in range(SW_GRP)], axis=0)
        kc = k_ref[0:CTX, :]
        vc = v_ref[0:CTX, :]
        lane_c = lax.broadcasted_iota(jnp.int32, (CTX, 256), 1) // HEAD_DIM
        if is_lat:
            j = blk - CTX // SW_QB
            start = pl.multiple_of(jnp.minimum(CTX + (j - 1) * SW_QB, T - nwin), SW_QB)
            kw = k_ref[pl.ds(start, nwin), :]
            vw = v_ref[pl.ds(start, nwin), :]
            lane_w = lax.broadcasted_iota(jnp.int32, (nwin, 256), 1) // HEAD_DIM
            kpos = start - CTX + lax.broadcasted_iota(jnp.int32, (nwin, nq), 0)
            qpos = j * SW_QB + lax.broadcasted_iota(jnp.int32, (nwin, nq), 1) % SW_QB
            band = (jnp.abs(qpos - kpos) <= SW_WIN) & (kpos >= 0)
        nkeys = CTX + nwin if is_lat else CTX
        for h in range(SW_KV):
            kcz = jnp.where(lane_c == h, kc, jnp.zeros_like(kc))
            s_scr[h, 0:CTX, :] = lax.dot_general(kcz, qs, NT_DIMS, preferred_element_type=F32)
            if is_lat:
                kwz = jnp.where(lane_w == h, kw, jnp.zeros_like(kw))
                s_w = lax.dot_general(kwz, qs, NT_DIMS, preferred_element_type=F32)
                s_scr[h, CTX:nkeys, :] = jnp.where(band, s_w, NEG)
        vall = jnp.concatenate([vc, vw], axis=0) if is_lat else vc
        for h in range(SW_KV):
            sink = jnp.zeros((1, nq), F32)
            for g in range(SW_GRP):
                sg = sink_ref[0:1, SW_GRP * h + g:SW_GRP * h + g + 1] * LOG2E
                sink = jnp.where(grp == g, sg, sink)
            s = s_scr[h, 0:nkeys, :]
            m = jnp.maximum(jnp.max(s, axis=0, keepdims=True), sink)
            e = jnp.exp2(s - m)
            r = 1.0 / (jnp.sum(e, axis=0, keepdims=True) + jnp.exp2(sink - m))
            ot = lax.dot_general(vall, (e * r).astype(BF16), TN_DIMS, preferred_element_type=F32)
            rows = slice(h * HEAD_DIM, (h + 1) * HEAD_DIM)
            acc_scr[rows, :] = ot[rows, :]
        for g in range(SW_GRP):
            o_ref[:, g * 256:(g + 1) * 256] = acc_scr[:, g * SW_QB:(g + 1) * SW_QB].T.astype(BF16)

    pl.when(blk < CTX // SW_QB)(lambda: body(False))
    pl.when(blk >= CTX // SW_QB)(lambda: body(True))


def _swa_attn(qkv, sink):
    return pl.pallas_call(
        _swa_kernel,
        grid=(B, T // SW_QB),
        in_specs=[_resident((1, 16)),
                  pl.BlockSpec((None, SW_QB, D), lambda b, i: (b, i, 0)),
                  pl.BlockSpec((None, T, 256), lambda b, i: (b, 0, 4)),
                  pl.BlockSpec((None, T, 256), lambda b, i: (b, 0, 5))],
        out_specs=pl.BlockSpec((None, SW_QB, D), lambda b, i: (b, i, 0)),
        out_shape=jax.ShapeDtypeStruct((B, T, D), BF16),
        scratch_shapes=[pltpu.VMEM((SW_KV, CTX + 3 * SW_QB, SW_GRP * SW_QB), F32),
                        pltpu.VMEM((SW_KV * HEAD_DIM, SW_GRP * SW_QB), F32)],
        compiler_params=_cparams("parallel", "parallel"),
        name="swa_attn",
    )(sink, qkv, qkv, qkv)


def _logsig(x):
    return jnp.minimum(x, 0.0) - jnp.log(1.0 + jnp.exp(-jnp.abs(x)))


def _mlstm_kernel(qkf_ref, vf_ref, gcf_ref, grf_ref, qkb_ref, vb_ref, gcb_ref, grb_ref,
                  hf_ref, hb_ref, st_ref, m_ref):
    step = pl.program_id(1)

    @pl.when(step == 0)
    def _():
        st_ref[...] = jnp.zeros_like(st_ref)
        m_ref[...] = jnp.zeros_like(m_ref)

    L = TM
    row = lax.broadcasted_iota(jnp.int32, (L, L), 0)
    col = lax.broadcasted_iota(jnp.int32, (L, L), 1)
    lower = col <= row
    upper = col >= row
    lower_f = lower.astype(F32)
    upper_f = upper.astype(F32)
    lane = lax.broadcasted_iota(jnp.int32, (L, 128), 1)
    ones_blk = jnp.where(lane == 0, 1.0, 0.0).astype(BF16)

    dirs = ((qkf_ref, vf_ref, gcf_ref, grf_ref, hf_ref), (qkb_ref, vb_ref, gcb_ref, grb_ref, hb_ref))
    for d, (qk_ref, v_ref, gc_ref, gr_ref, h_ref) in enumerate(dirs):
        tri = lower if d == 0 else upper
        tri_f, tri_t_f = (lower_f, upper_f) if d == 0 else (upper_f, lower_f)
        last = L - 1 if d == 0 else 0
        gc = gc_ref[...]
        gr = gr_ref[...]
        lf_c = _logsig(gc[:, 16 * d + 8:16 * d + 16])
        lf_r = _logsig(gr[16 * d + 8:16 * d + 16, :])
        cum_c = jnp.dot(tri_f, lf_c, precision=HIGHEST, preferred_element_type=F32)
        cum_r = jnp.dot(lf_r, tri_t_f, precision=HIGHEST, preferred_element_type=F32)
        for h in range(ML_HEADS):
            idx = d * ML_HEADS + h
            p = h // 2
            q2 = qk_ref[:, p * 128:(p + 1) * 128]
            k2 = qk_ref[:, 512 + p * 128:512 + (p + 1) * 128]
            own = lane >= HEAD_DIM if h % 2 else lane < HEAD_DIM
            qh = jnp.where(own, q2, jnp.zeros_like(q2))
            vh = v_ref[:, h * ML_V:(h + 1) * ML_V]
            vaug = jnp.concatenate([vh, ones_blk], axis=1)
            b_c = cum_c[:, h:h + 1]
            b_r = cum_r[h:h + 1, :]
            i_c = gc[:, 16 * d + h:16 * d + h + 1]
            i_r = gr[16 * d + h:16 * d + h + 1, :]
            m_prev = m_ref[idx][0:1, 0:1]
            st = st_ref[idx]

            dm = jnp.where(tri, b_c - b_r + i_r, NEG)
            a_c = b_c + m_prev
            m_t = jnp.maximum(a_c, jnp.max(dm, axis=-1, keepdims=True))
            inter = jnp.exp(a_c - m_t)
            sc = lax.dot_general(qh, k2, NT_DIMS, preferred_element_type=F32) * jnp.exp(dm - m_t)
            carry = jnp.dot(qh, st.astype(BF16), preferred_element_type=F32)
            num = inter * carry[:, 0:ML_V] + jnp.dot(sc.astype(BF16), vh, preferred_element_type=F32)
            den = inter * carry[:, ML_V:ML_V + 1] + jnp.sum(sc, axis=-1, keepdims=True)
            h_ref[:, h * ML_V:(h + 1) * ML_V] = num / jnp.maximum(jnp.abs(den), jnp.exp(-m_t))

            m_new = m_t[last:last + 1, :]
            b_last = b_c[last:last + 1, :]
            g_c = jnp.exp(b_last - b_c + i_c - m_new)
            decay = jnp.exp(b_last + m_prev - m_new)
            gv = (g_c * vaug.astype(F32)).astype(BF16)
            st_ref[idx] = decay * st + lax.dot_general(k2, gv, TN_DIMS, preferred_element_type=F32)
            m_ref[idx] = jnp.broadcast_to(m_new, (8, 128))


def _mlstm(qk, v, gc, gr):
    fwd = lambda b, s: (b, s, 0)
    bwd = lambda b, s: (b, jnp.where(s == 0, 0, NT - s), 0)
    fwd_t = lambda b, s: (b, 0, s)
    bwd_t = lambda b, s: (b, 0, jnp.where(s == 0, 0, NT - s))
    tok = lambda n, ix: pl.BlockSpec((None, TM, n), ix)
    return pl.pallas_call(
        _mlstm_kernel,
        grid=(B, NT),
        in_specs=[tok(D, fwd), tok(D, fwd), tok(32, fwd), pl.BlockSpec((None, 32, TM), fwd_t),
                  tok(D, bwd), tok(D, bwd), tok(32, bwd), pl.BlockSpec((None, 32, TM), bwd_t)],
        out_specs=[tok(D, fwd), tok(D, bwd)],
        out_shape=[jax.ShapeDtypeStruct((B, T, D), F32)] * 2,
        scratch_shapes=[pltpu.VMEM((2 * ML_HEADS, 128, 256), F32),
                        pltpu.VMEM((2 * ML_HEADS, 8, 128), F32)],
        compiler_params=_cparams("parallel", "arbitrary"),
        name="mlstm",
    )(qk, v, gc, gr, qk, v, gc, gr)


def _out_proj_kernel(a_ref, w_ref, x_ref, mod_ref, o_ref):
    y = jnp.dot(a_ref[...], w_ref[...], preferred_element_type=F32)
    o_ref[...] = x_ref[...] + mod_ref[2:3, :] * y


def _out_proj(a, w, xs, mod, need_ctx):
    off = 0 if need_ctx else 1
    n = NT - off
    return pl.pallas_call(
        _out_proj_kernel,
        grid=(B, n),
        in_specs=[pl.BlockSpec((None, TM, D), lambda b, i: (b, i, 0)),
                  _resident((D, D)),
                  pl.BlockSpec((None, TM, D), lambda b, i: (b, i + off, 0)),
                  pl.BlockSpec((None, 6, D), _mod_ix_ctx if need_ctx else _mod_ix_lat)],
        out_specs=pl.BlockSpec((None, TM, D), lambda b, i: (b, i, 0)),
        out_shape=jax.ShapeDtypeStruct((B, n * TM, D), F32),
        compiler_params=_cparams("parallel", "parallel"),
        name="out_proj",
    )(a, w, xs, mod)


def _ml_out_kernel(hf_ref, hb_ref, og_ref, ng_ref, w_ref, x_ref, mod_ref, o_ref):
    parts = []
    for h in range(ML_HEADS):
        sl = slice(h * ML_V, (h + 1) * ML_V)
        y = _rms(hf_ref[:, sl] + hb_ref[:, sl], ng_ref[:, sl])
        parts.append((y * jax.nn.sigmoid(og_ref[:, sl])).astype(BF16))
    y = jnp.dot(jnp.concatenate(parts, axis=1), w_ref[...], preferred_element_type=F32)
    o_ref[...] = x_ref[...] + mod_ref[2:3, :] * y


def _ml_out(hf, hb, og, ng, w, xs, mod):
    tile = pl.BlockSpec((None, TM, D), lambda b, i: (b, i, 0))
    return pl.pallas_call(
        _ml_out_kernel,
        grid=(B, NT),
        in_specs=[tile, tile, tile, _resident((1, D)), _resident((D, D)), tile,
                  pl.BlockSpec((None, 6, D), _mod_ix_ctx)],
        out_specs=tile,
        out_shape=jax.ShapeDtypeStruct((B, T, D), F32),
        compiler_params=_cparams("parallel", "parallel"),
        name="ml_out",
    )(hf, hb, og, ng, w, xs, mod)


def _ffn_kernel(x_ref, xp_ref, xn_ref, mod_ref, g_ref, wup_ref, cw_ref, cb_ref, wdn_ref, fg_ref,
                o_ref, h_scr, ua_scr, ug_scr, act_scr, *, has_ctx, nt, final):
    i = pl.program_id(1)
    g = g_ref[...]
    shift, scale, gate = mod_ref[3:4, :], mod_ref[4:5, :], mod_ref[5:6, :]
    if has_ctx:
        prev_ok = i >= 2
        next_ok = (i != 0) & (i != nt - 1)
    else:
        prev_ok = i >= 1
        next_ok = i != nt - 1
    x = x_ref[...]
    h_scr[0:HALO, :] = jnp.where(prev_ok, _norm_mod(xp_ref[...], g, shift, scale), 0.0)
    h_scr[HALO:HALO + TM, :] = _norm_mod(x, g, shift, scale)
    h_scr[HALO + TM:, :] = jnp.where(next_ok, _norm_mod(xn_ref[...], g, shift, scale), 0.0)
    hb = h_scr[...].astype(BF16)

    def conv(scr, off):
        w = cw_ref[:, off:off + FFN_CHUNK]
        return (scr[HALO - 1:HALO - 1 + TM, :] * w[0:1] + scr[HALO:HALO + TM, :] * w[1:2]
                + scr[HALO + 1:HALO + 1 + TM, :] * w[2:3] + cb_ref[:, off:off + FFN_CHUNK])

    for c in range(FFN // FFN_CHUNK):
        lo = c * FFN_CHUNK
        ua_scr[c % 2] = jnp.dot(hb, wup_ref[:, lo:lo + FFN_CHUNK], preferred_element_type=F32)
        ug_scr[c % 2] = jnp.dot(hb, wup_ref[:, FFN + lo:FFN + lo + FFN_CHUNK], preferred_element_type=F32)
        a = conv(ua_scr.at[c % 2], lo)
        gg = conv(ug_scr.at[c % 2], FFN + lo)
        act_scr[:, lo:lo + FFN_CHUNK] = (a * (gg * jax.nn.sigmoid(gg))).astype(BF16)
    y = jnp.dot(act_scr[...], wdn_ref[...], preferred_element_type=F32)
    out = x + gate * y
    if final:
        out = _rms(out, fg_ref[...])
    o_ref[...] = out


def _ffn(xs, mod, g, wup, cw, cb, wdn, fg, has_ctx, final):
    rows = xs.shape[1]
    nt = rows // TM
    per_tile = TM // HALO
    last_halo = rows // HALO - 1
    return pl.pallas_call(
        functools.partial(_ffn_kernel, has_ctx=has_ctx, nt=nt, final=final),
        grid=(B, nt),
        in_specs=[pl.BlockSpec((None, TM, D), lambda b, i: (b, i, 0)),
                  pl.BlockSpec((None, HALO, D), lambda b, i: (b, jnp.maximum(i * per_tile - 1, 0), 0)),
                  pl.BlockSpec((None, HALO, D), lambda b, i: (b, jnp.minimum((i + 1) * per_tile, last_halo), 0)),
                  pl.BlockSpec((None, 6, D), _mod_ix_ctx if has_ctx else _mod_ix_lat),
                  _resident((1, D)),
                  _resident((D, 2 * FFN)),
                  _resident((3, 2 * FFN)),
                  _resident((1, 2 * FFN)),
                  _resident((FFN, D)),
                  _resident((1, D))],
        out_specs=pl.BlockSpec((None, TM, D), lambda b, i: (b, i, 0)),
        out_shape=jax.ShapeDtypeStruct((B, rows, D), F32),
        scratch_shapes=[pltpu.VMEM((TM + 2 * HALO, D), F32),
                        pltpu.VMEM((2, TM + 2 * HALO, FFN_CHUNK), F32),
                        pltpu.VMEM((2, TM + 2 * HALO, FFN_CHUNK), F32),
                        pltpu.VMEM((TM, FFN), BF16)],
        compiler_params=_cparams("parallel", "parallel"),
        name="ffn",
    )(xs, xs, xs, mod, g, wup, cw, cb, wdn, fg)


def _rope_tables():
    rows = S // GRID_W
    quarter = HEAD_DIM // 4
    row = jnp.repeat(jnp.arange(rows, dtype=F32), GRID_W)
    col = jnp.tile(jnp.arange(GRID_W, dtype=F32), rows)
    inv = 10000.0 ** (-jnp.arange(quarter, dtype=F32) / quarter)
    ang_r = row[:, None] * inv
    ang_c = col[:, None] * inv
    cr, sr, cc, sc = jnp.cos(ang_r), jnp.sin(ang_r), jnp.cos(ang_c), jnp.sin(ang_c)
    z = jnp.zeros_like(sr)
    cos = jnp.concatenate([cr, cr, cc, cc], axis=1)
    sa = jnp.concatenate([-sr, z, -sc, z], axis=1)
    sb = jnp.concatenate([z, sr, z, sc], axis=1)
    pad = lambda t, v: jnp.concatenate([jnp.full((CTX, 128), v, F32), jnp.tile(t, (1, 2))], axis=0)
    return pad(cos, 1.0), pad(sa, 0.0), pad(sb, 0.0)


def _group_major(w, axis):
    shape = w.shape
    w = w.reshape(shape[:axis] + (SW_KV, SW_GRP, HEAD_DIM) + shape[axis + 1:])
    w = jnp.swapaxes(w, axis, axis + 1)
    return w.reshape(shape)


def _row(v):
    return v.reshape(1, -1)


def _bf(w):
    return w.astype(BF16)


def _mod(cpad, w, b):
    return _ada(cpad, w, b).reshape(MOD_ROWS, 6, D)


def _da_layer(idx, xs, mod, rope, norm1_g, w_in, lq1, lk1, lq2, lk2, subln_g, w_out):
    need_ctx = idx < DEPTH - 1
    lam_init = 0.8 - 0.6 * math.exp(-0.3 * idx)
    qkv = _attn_proj(xs, mod, _row(norm1_g), _bf(w_in), rope, 4, 4, 4)
    o = _da_attn(qkv, jnp.stack([lq1, lk1, lq2, lk2]), _row(subln_g), lam_init, need_ctx)
    return _out_proj(o, _bf(w_out), xs, mod, need_ctx)


def _ml_layer(xs, mod, norm1_g, w_in, gate_b, norm_g, w_out):
    wg = _bf(w_in[:, 3072:])
    qk, v, og, gc, gr = _ml_proj(xs, mod, _row(norm1_g), _bf(w_in[:, :3072]), wg, wg.T,
                                 _row(gate_b), gate_b.reshape(-1, 1))
    hf, hb = _mlstm(qk, v, gc, gr)
    return _ml_out(hf, hb, og, _row(norm_g), _bf(w_out), xs, mod)


def _sw_layer(xs, mod, rope, norm1_g, w_in, sink, w_out):
    w_q = _group_major(w_in[:, :D], 1)
    qkv = _attn_proj(xs, mod, _row(norm1_g), _bf(jnp.concatenate([w_q, w_in[:, D:]], axis=1)), rope, 4, 1, 1)
    o = _swa_attn(qkv, _row(sink))
    return _out_proj(o, _bf(_group_major(w_out, 0)), xs, mod, True)


def _ffn_layer(idx, xs, mod, norm2_g, w_up, conv_w, conv_b, w_down, final_norm_g):
    final = idx == DEPTH - 1
    return _ffn(xs, mod, _row(norm2_g), _bf(w_up), conv_w, _row(conv_b), _bf(w_down),
                _row(final_norm_g), has_ctx=not final, final=final)


def kernel(x, c, ctx, c_ctx, l0_ada_w, l0_ada_b, l0_norm1_g, l0_da_w_in, l0_da_lam_q1, l0_da_lam_k1, l0_da_lam_q2, l0_da_lam_k2, l0_da_subln_g, l0_da_w_out, l0_norm2_g, l0_ffn_w_up, l0_ffn_conv_w, l0_ffn_conv_b, l0_ffn_w_down, l1_ada_w, l1_ada_b, l1_norm1_g, l1_ml_w_in, l1_ml_gate_b, l1_ml_norm_g, l1_ml_w_out, l1_norm2_g, l1_ffn_w_up, l1_ffn_conv_w, l1_ffn_conv_b, l1_ffn_w_down, l2_ada_w, l2_ada_b, l2_norm1_g, l2_sw_w_in, l2_sw_sink, l2_sw_w_out, l2_norm2_g, l2_ffn_w_up, l2_ffn_conv_w, l2_ffn_conv_b, l2_ffn_w_down, l3_ada_w, l3_ada_b, l3_norm1_g, l3_da_w_in, l3_da_lam_q1, l3_da_lam_k1, l3_da_lam_q2, l3_da_lam_k2, l3_da_subln_g, l3_da_w_out, l3_norm2_g, l3_ffn_w_up, l3_ffn_conv_w, l3_ffn_conv_b, l3_ffn_w_down, final_norm_g):
    rope = _rope_tables()
    xs = jnp.concatenate([ctx, x], axis=1)
    cpad = jnp.concatenate([c, c_ctx[None, :], jnp.zeros((MOD_ROWS - B - 1, D), F32)], axis=0)

    m = _mod(cpad, l0_ada_w, l0_ada_b)
    xs = _da_layer(0, xs, m, rope, l0_norm1_g, l0_da_w_in, l0_da_lam_q1, l0_da_lam_k1, l0_da_lam_q2, l0_da_lam_k2,
                   l0_da_subln_g, l0_da_w_out)
    xs = _ffn_layer(0, xs, m, l0_norm2_g, l0_ffn_w_up, l0_ffn_conv_w, l0_ffn_conv_b, l0_ffn_w_down, final_norm_g)

    m = _mod(cpad, l1_ada_w, l1_ada_b)
    xs = _ml_layer(xs, m, l1_norm1_g, l1_ml_w_in, l1_ml_gate_b, l1_ml_norm_g, l1_ml_w_out)
    xs = _ffn_layer(1, xs, m, l1_norm2_g, l1_ffn_w_up, l1_ffn_conv_w, l1_ffn_conv_b, l1_ffn_w_down, final_norm_g)

    m = _mod(cpad, l2_ada_w, l2_ada_b)
    xs = _sw_layer(xs, m, rope, l2_norm1_g, l2_sw_w_in, l2_sw_sink, l2_sw_w_out)
    xs = _ffn_layer(2, xs, m, l2_norm2_g, l2_ffn_w_up, l2_ffn_conv_w, l2_ffn_conv_b, l2_ffn_w_down, final_norm_g)

    m = _mod(cpad, l3_ada_w, l3_ada_b)
    xs = _da_layer(3, xs, m, rope, l3_norm1_g, l3_da_w_in, l3_da_lam_q1, l3_da_lam_k1, l3_da_lam_q2, l3_da_lam_k2,
                   l3_da_subln_g, l3_da_w_out)
    return _ffn_layer(3, xs, m, l3_norm2_g, l3_ffn_w_up, l3_ffn_conv_w, l3_ffn_conv_b, l3_ffn_w_down, final_norm_g)
```

```python
import functools
import math

import jax
import jax.numpy as jnp
from jax import lax
from jax.experimental import pallas as pl
from jax.experimental.pallas import tpu as pltpu

F32 = jnp.float32
BF16 = jnp.bfloat16
HIGHEST = lax.Precision.HIGHEST

D = 1024
B = 8
S = 2048
CTX = 256
T = CTX + S
DEPTH = 4
GRID_W = 64
HEAD_DIM = 64
EPS = 1e-6
NEG = -1e30

TM = 256
NT = T // TM
NS = S // TM
MOD_ROWS = 16

DA_HEADS = 8
DA_HPS = 4
ML_HEADS = 8
ML_V = 128
SW_KV = 4
SW_GRP = 4
SW_WIN = 128
SW_QB = 128
FFN = 2816
FFN_CHUNK = 256
HALO = 8
FFN_EDGE = 16

LOG2E = math.log2(math.e)
Q_SCALE = HEAD_DIM ** -0.5 * LOG2E

NT_DIMS = (((1,), (1,)), ((), ()))
TN_DIMS = (((0,), (0,)), ((), ()))

VMEM_LIMIT = 56 * 1024 * 1024


def _cparams(*sem):
    return pltpu.CompilerParams(dimension_semantics=sem, vmem_limit_bytes=VMEM_LIMIT)


def _resident(shape):
    zeros = (0,) * len(shape)
    return pl.BlockSpec(shape, lambda *_: zeros, pipeline_mode=pl.Buffered(1))


def _rms(x, g):
    ms = jnp.mean(x * x, axis=-1, keepdims=True)
    return x * lax.rsqrt(ms + EPS) * g


def _norm_mod(x, g, shift, scale):
    return _rms(x, g) * (1.0 + scale) + shift


def _mod_ix_ctx(b, i):
    return (jnp.where(i == 0, B, b), 0, 0)


def _mod_ix_lat(b, i):
    return (b, 0, 0)


def _ada_kernel(c_ref, w_ref, b_ref, o_ref):
    c = c_ref[...]
    a = (c * jax.nn.sigmoid(c)).astype(BF16)
    o_ref[...] = jnp.dot(a, w_ref[...].astype(BF16), preferred_element_type=F32) + b_ref[...]


def _ada(cpad, w, bias):
    n = w.shape[1]
    tn = 1024
    return pl.pallas_call(
        _ada_kernel,
        grid=(n // tn,),
        in_specs=[pl.BlockSpec((MOD_ROWS, D), lambda j: (0, 0)),
                  pl.BlockSpec((D, tn), lambda j: (0, j)),
                  pl.BlockSpec((1, tn), lambda j: (0, j))],
        out_specs=pl.BlockSpec((MOD_ROWS, tn), lambda j: (0, j)),
        out_shape=jax.ShapeDtypeStruct((MOD_ROWS, n), F32),
        compiler_params=_cparams("arbitrary"),
        name="ada",
    )(cpad, w, bias.reshape(1, n))


def _rope(c, cos, sa, sb):
    return c * cos + pltpu.roll(c, 112, 1) * sa + pltpu.roll(c, 16, 1) * sb


def _attn_proj_kernel(x_ref, mod_ref, g_ref, w_ref, cos_ref, sa_ref, sb_ref, o_ref, *, nq, nk, nv):
    h = _norm_mod(x_ref[...], g_ref[...], mod_ref[0:1, :], mod_ref[1:2, :]).astype(BF16)
    cos, sa, sb = cos_ref[...], sa_ref[...], sb_ref[...]
    for j in range(nq + nk + nv):
        y = jnp.dot(h, w_ref[:, j * 256:(j + 1) * 256], preferred_element_type=F32)
        if j < nq + nk:
            halves = []
            for t in range(2):
                r = _rope(y[:, t * 128:(t + 1) * 128], cos, sa, sb)
                halves.append(r * Q_SCALE if j < nq else r)
            y = jnp.concatenate(halves, axis=1)
        o_ref[:, j * 256:(j + 1) * 256] = y.astype(BF16)


def _attn_proj(xs, mod, g, w, rope, nq, nk, nv):
    n = (nq + nk + nv) * 256
    cos, sa, sb = rope
    tab = pl.BlockSpec((TM, 128), lambda b, i: (i, 0))
    return pl.pallas_call(
        functools.partial(_attn_proj_kernel, nq=nq, nk=nk, nv=nv),
        grid=(B, NT),
        in_specs=[pl.BlockSpec((None, TM, D), lambda b, i: (b, i, 0)),
                  pl.BlockSpec((None, 6, D), _mod_ix_ctx),
                  _resident((1, D)),
                  _resident((D, n)),
                  tab, tab, tab],
        out_specs=pl.BlockSpec((None, TM, n), lambda b, i: (b, i, 0)),
        out_shape=jax.ShapeDtypeStruct((B, T, n), BF16),
        compiler_params=_cparams("parallel", "parallel"),
        name="attn_proj",
    )(xs, mod, g, w, cos, sa, sb)


def _ml_proj_kernel(x_ref, mod_ref, g_ref, w_ref, wg_ref, wgt_ref, gb_ref, gbt_ref,
                    qk_ref, vt_ref, o_ref, gc_ref, gr_ref):
    h = _norm_mod(x_ref[...], g_ref[...], mod_ref[0:1, :], mod_ref[1:2, :]).astype(BF16)
    for j in range(12):
        y = jnp.dot(h, w_ref[:, j * 256:(j + 1) * 256], preferred_element_type=F32)
        if j < 2:
            qk_ref[:, j * 256:(j + 1) * 256] = (y * 0.125).astype(BF16)
        elif j < 4:
            qk_ref[:, j * 256:(j + 1) * 256] = y.astype(BF16)
        elif j < 8:
            vt_ref[(j - 4) * 256:(j - 3) * 256, :] = y.T.astype(BF16)
        else:
            o_ref[:, (j - 8) * 256:(j - 7) * 256] = y
    gc_ref[...] = jnp.dot(h, wg_ref[...], preferred_element_type=F32) + gb_ref[...]
    gr_ref[...] = lax.dot_general(wgt_ref[...], h, NT_DIMS, preferred_element_type=F32) + gbt_ref[...]


def _ml_proj(xs, mod, g, w, wg, wgt, gb, gbt):
    tile = lambda n, dt: (pl.BlockSpec((None, TM, n), lambda b, i: (b, i, 0)),
                          jax.ShapeDtypeStruct((B, T, n), dt))
    tile_t = lambda n, dt: (pl.BlockSpec((None, n, TM), lambda b, i: (b, 0, i)),
                            jax.ShapeDtypeStruct((B, n, T), dt))
    outs = [tile(D, BF16), tile_t(D, BF16), tile(D, F32), tile(32, F32), tile_t(32, F32)]
    return pl.pallas_call(
        _ml_proj_kernel,
        grid=(B, NT),
        in_specs=[pl.BlockSpec((None, TM, D), lambda b, i: (b, i, 0)),
                  pl.BlockSpec((None, 6, D), _mod_ix_ctx),
                  _resident((1, D)),
                  _resident((D, 3072)),
                  _resident((D, 32)),
                  _resident((32, D)),
                  _resident((1, 32)),
                  _resident((32, 1))],
        out_specs=[o[0] for o in outs],
        out_shape=[o[1] for o in outs],
        compiler_params=_cparams("parallel", "parallel"),
        name="ml_proj",
    )(xs, mod, g, w, wg, wgt, gb, gbt)


def _da_attn_kernel(lam_ref, g_ref, q_ref, k_ref, v_ref, o_ref, s_scr, *, lam_init, has_ctx_tile):
    i = pl.program_id(2)
    lv = lam_ref[...]
    lam = (jnp.exp(jnp.sum(lv[0:1] * lv[1:2], axis=-1, keepdims=True))
           - jnp.exp(jnp.sum(lv[2:3] * lv[3:4], axis=-1, keepdims=True)) + lam_init)
    lane = lax.broadcasted_iota(jnp.int32, (TM, 128), 1)

    def attend(nk):
        def scores(hh):
            sl = slice(hh * 128, (hh + 1) * 128)
            q = q_ref[:, sl]
            for c in range(2):
                own = lane < HEAD_DIM if c == 0 else lane >= HEAD_DIM
                s_scr[2 * hh + c, 0:nk, :] = lax.dot_general(
                    k_ref[0:nk, sl], jnp.where(own, q, jnp.zeros_like(q)), NT_DIMS, preferred_element_type=F32)

        scores(0)
        for hh in range(DA_HPS):
            if hh + 1 < DA_HPS:
                scores(hh + 1)
            sl = slice(hh * 128, (hh + 1) * 128)
            es, ls = [], []
            for c in range(2):
                s = s_scr[2 * hh + c, 0:nk, :]
                e = jnp.exp2(s - jnp.max(s, axis=0, keepdims=True))
                es.append(e)
                ls.append(jnp.sum(e, axis=0, keepdims=True))
            wt = es[0] - es[1] * (lam * ls[0] / ls[1])
            ot = lax.dot_general(v_ref[0:nk, sl], wt.astype(BF16), TN_DIMS, preferred_element_type=F32)
            o = (ot * (1.0 / ls[0])).T
            o_ref[:, sl] = (_rms(o, g_ref[...]) * (1.0 - lam_init)).astype(BF16)

    if has_ctx_tile:
        pl.when(i == 0)(lambda: attend(CTX))
        pl.when(i > 0)(lambda: attend(T))
    else:
        attend(T)


def _da_attn(qkv, lamv, subln_g, lam_init, need_ctx):
    off = 0 if need_ctx else 1
    nq = NT - off
    ng = DA_HEADS // DA_HPS
    wl = DA_HPS * 128
    return pl.pallas_call(
        functools.partial(_da_attn_kernel, lam_init=lam_init, has_ctx_tile=need_ctx),
        grid=(B, ng, nq),
        in_specs=[_resident((4, HEAD_DIM)),
                  _resident((1, 128)),
                  pl.BlockSpec((None, TM, wl), lambda b, h, i: (b, i + off, h)),
                  pl.BlockSpec((None, T, wl), lambda b, h, i: (b, 0, ng + h)),
                  pl.BlockSpec((None, T, wl), lambda b, h, i: (b, 0, 2 * ng + h))],
        out_specs=pl.BlockSpec((None, TM, wl), lambda b, h, i: (b, i, h)),
        out_shape=jax.ShapeDtypeStruct((B, nq * TM, D), BF16),
        scratch_shapes=[pltpu.VMEM((2 * DA_HPS, T, TM), F32)],
        compiler_params=_cparams("parallel", "parallel", "parallel"),
        name="da_attn",
    )(lamv, subln_g, qkv, qkv, qkv)


def _swa_kernel(sink_ref, q_ref, k_ref, v_ref, o_ref, s_scr, acc_scr):
    blk = pl.program_id(1)
    nwin = 3 * SW_QB
    nq = SW_GRP * SW_QB
    grp = lax.broadcasted_iota(jnp.int32, (1, nq), 1) // SW_QB

    def body(is_lat):
        qs = jnp.concatenate([q_ref[:, g * 256:(g + 1) * 256] for g in range(SW_GRP)], axis=0)
        kc = k_ref[0:CTX, :]
        vc = v_ref[0:CTX, :]
        lane_c = lax.broadcasted_iota(jnp.int32, (CTX, 256), 1) // HEAD_DIM
        if is_lat:
            j = blk - CTX // SW_QB
            start = pl.multiple_of(jnp.minimum(CTX + (j - 1) * SW_QB, T - nwin), SW_QB)
            kw = k_ref[pl.ds(start, nwin), :]
            vw = v_ref[pl.ds(start, nwin), :]
            lane_w = lax.broadcasted_iota(jnp.int32, (nwin, 256), 1) // HEAD_DIM
            kpos = start - CTX + lax.broadcasted_iota(jnp.int32, (nwin, nq), 0)
            qpos = j * SW_QB + lax.broadcasted_iota(jnp.int32, (nwin, nq), 1) % SW_QB
            band = (jnp.abs(qpos - kpos) <= SW_WIN) & (kpos >= 0)
        nkeys = CTX + nwin if is_lat else CTX
        for h in range(SW_KV):
            kcz = jnp.where(lane_c == h, kc, jnp.zeros_like(kc))
            s_scr[h, 0:CTX, :] = lax.dot_general(kcz, qs, NT_DIMS, preferred_element_type=F32)
            if is_lat:
                kwz = jnp.where(lane_w == h, kw, jnp.zeros_like(kw))
                s_w = lax.dot_general(kwz, qs, NT_DIMS, preferred_element_type=F32)
                s_scr[h, CTX:nkeys, :] = jnp.where(band, s_w, NEG)
        vall = jnp.concatenate([vc, vw], axis=0) if is_lat else vc
        for h in range(SW_KV):
            sink = jnp.zeros((1, nq), F32)
            for g in range(SW_GRP):
                sg = sink_ref[0:1, SW_GRP * h + g:SW_GRP * h + g + 1] * LOG2E
                sink = jnp.where(grp == g, sg, sink)
            s = s_scr[h, 0:nkeys, :]
            m = jnp.maximum(jnp.max(s, axis=0, keepdims=True), sink)
            e = jnp.exp2(s - m)
            r = 1.0 / (jnp.sum(e, axis=0, keepdims=True) + jnp.exp2(sink - m))
            ot = lax.dot_general(vall, (e * r).astype(BF16), TN_DIMS, preferred_element_type=F32)
            rows = slice(h * HEAD_DIM, (h + 1) * HEAD_DIM)
            acc_scr[rows, :] = ot[rows, :]
        for g in range(SW_GRP):
            o_ref[:, g * 256:(g + 1) * 256] = acc_scr[:, g * SW_QB:(g + 1) * SW_QB].T.astype(BF16)

    pl.when(blk < CTX // SW_QB)(lambda: body(False))
    pl.when(blk >= CTX // SW_QB)(lambda: body(True))


def _swa_attn(qkv, sink):
    return pl.pallas_call(
        _swa_kernel,
        grid=(B, T // SW_QB),
        in_specs=[_resident((1, 16)),
                  pl.BlockSpec((None, SW_QB, D), lambda b, i: (b, i, 0)),
                  pl.BlockSpec((None, T, 256), lambda b, i: (b, 0, 4)),
                  pl.BlockSpec((None, T, 256), lambda b, i: (b, 0, 5))],
        out_specs=pl.BlockSpec((None, SW_QB, D), lambda b, i: (b, i, 0)),
        out_shape=jax.ShapeDtypeStruct((B, T, D), BF16),
        scratch_shapes=[pltpu.VMEM((SW_KV, CTX + 3 * SW_QB, SW_GRP * SW_QB), F32),
                        pltpu.VMEM((SW_KV * HEAD_DIM, SW_GRP * SW_QB), F32)],
        compiler_params=_cparams("parallel", "parallel"),
        name="swa_attn",
    )(sink, qkv, qkv, qkv)


def _logsig(x):
    return jnp.minimum(x, 0.0) - jnp.log(1.0 + jnp.exp(-jnp.abs(x)))


def _mlstm_kernel(qkf_ref, vtf_ref, gcf_ref, grf_ref, qkb_ref, vtb_ref, gcb_ref, grb_ref,
                  hf_ref, hb_ref, st_ref, m_ref):
    step = pl.program_id(1)

    @pl.when(step == 0)
    def _():
        st_ref[...] = jnp.zeros_like(st_ref)
        m_ref[...] = jnp.zeros_like(m_ref)

    L = TM
    row = lax.broadcasted_iota(jnp.int32, (L, L), 0)
    col = lax.broadcasted_iota(jnp.int32, (L, L), 1)
    lower = col <= row
    upper = col >= row
    lower_f = lower.astype(F32)
    upper_f = upper.astype(F32)
    lane = lax.broadcasted_iota(jnp.int32, (L, 128), 1)
    ones_t = jnp.where(lax.broadcasted_iota(jnp.int32, (ML_V, L), 0) == 0, 1.0, 0.0).astype(BF16)

    dirs = ((qkf_ref, vtf_ref, gcf_ref, grf_ref, hf_ref), (qkb_ref, vtb_ref, gcb_ref, grb_ref, hb_ref))
    for d, (qk_ref, vt_ref, gc_ref, gr_ref, h_ref) in enumerate(dirs):
        vis = upper if d == 0 else lower
        vis_f, vis_t_f = (upper_f, lower_f) if d == 0 else (lower_f, upper_f)
        last = L - 1 if d == 0 else 0
        gc = gc_ref[...]
        gr = gr_ref[...]
        lf_c = _logsig(gc[:, 16 * d + 8:16 * d + 16])
        lf_r = _logsig(gr[16 * d + 8:16 * d + 16, :])
        cum_r = jnp.dot(lf_r, vis_f, precision=HIGHEST, preferred_element_type=F32)
        cum_c = jnp.dot(vis_t_f, lf_c, precision=HIGHEST, preferred_element_type=F32)
        ci_all = gc[:, 16 * d:16 * d + 8] - cum_c
        for h in range(ML_HEADS):
            idx = d * ML_HEADS + h
            p = h // 2
            q2 = qk_ref[:, p * 128:(p + 1) * 128]
            k2 = qk_ref[:, 512 + p * 128:512 + (p + 1) * 128]
            own = lane >= HEAD_DIM if h % 2 else lane < HEAD_DIM
            qh = jnp.where(own, q2, jnp.zeros_like(q2))
            vt = vt_ref[h * ML_V:(h + 1) * ML_V, :]
            b_r = cum_r[h:h + 1, :]
            i_r = gr[16 * d + h:16 * d + h + 1, :]
            m_prev = m_ref[idx][0:1, 0:1]
            st = st_ref[idx]

            dm = jnp.where(vis, ci_all[:, h:h + 1] + b_r, NEG)
            a_r = b_r + m_prev
            m_r = jnp.maximum(a_r, jnp.max(dm, axis=0, keepdims=True))
            inter = jnp.exp(a_r - m_r)
            sc = lax.dot_general(k2, qh, NT_DIMS, preferred_element_type=F32) * jnp.exp(dm - m_r)
            carry = lax.dot_general(st.astype(BF16), qh, NT_DIMS, preferred_element_type=F32)
            num = inter * carry[0:ML_V, :] + jnp.dot(vt, sc.astype(BF16), preferred_element_type=F32)
            den = inter * carry[ML_V:ML_V + 1, :] + jnp.sum(sc, axis=0, keepdims=True)
            ht = num / jnp.maximum(jnp.abs(den), jnp.exp(-m_r))
            h_ref[:, h * ML_V:(h + 1) * ML_V] = ht.T

            m_new = m_r[:, last:last + 1]
            b_last = b_r[:, last:last + 1]
            g_r = jnp.exp(b_last - b_r + i_r - m_new)
            decay = jnp.exp(b_last + m_prev - m_new)
            gvt = (jnp.concatenate([vt, ones_t], axis=0).astype(F32) * g_r).astype(BF16)
            st_ref[idx] = decay * st + jnp.dot(gvt, k2, preferred_element_type=F32)
            m_ref[idx] = jnp.broadcast_to(m_new, (8, 128))


def _mlstm(qk, vt, gc, gr):
    fwd = lambda b, s: (b, s, 0)
    bwd = lambda b, s: (b, jnp.where(s == 0, 0, NT - s), 0)
    fwd_t = lambda b, s: (b, 0, s)
    bwd_t = lambda b, s: (b, 0, jnp.where(s == 0, 0, NT - s))
    tok = lambda n, ix: pl.BlockSpec((None, TM, n), ix)
    tok_t = lambda n, ix: pl.BlockSpec((None, n, TM), ix)
    return pl.pallas_call(
        _mlstm_kernel,
        grid=(B, NT),
        in_specs=[tok(D, fwd), tok_t(D, fwd_t), tok(32, fwd), tok_t(32, fwd_t),
                  tok(D, bwd), tok_t(D, bwd_t), tok(32, bwd), tok_t(32, bwd_t)],
        out_specs=[tok(D, fwd), tok(D, bwd)],
        out_shape=[jax.ShapeDtypeStruct((B, T, D), F32)] * 2,
        scratch_shapes=[pltpu.VMEM((2 * ML_HEADS, 2 * ML_V, 128), F32),
                        pltpu.VMEM((2 * ML_HEADS, 8, 128), F32)],
        compiler_params=_cparams("parallel", "arbitrary"),
        name="mlstm",
    )(qk, vt, gc, gr, qk, vt, gc, gr)


def _out_proj_kernel(a_ref, w_ref, x_ref, mod_ref, o_ref):
    y = jnp.dot(a_ref[...], w_ref[...], preferred_element_type=F32)
    o_ref[...] = x_ref[...] + mod_ref[2:3, :] * y


def _out_proj(a, w, xs, mod, need_ctx):
    off = 0 if need_ctx else 1
    n = NT - off
    return pl.pallas_call(
        _out_proj_kernel,
        grid=(B, n),
        in_specs=[pl.BlockSpec((None, TM, D), lambda b, i: (b, i, 0)),
                  _resident((D, D)),
                  pl.BlockSpec((None, TM, D), lambda b, i: (b, i + off, 0)),
                  pl.BlockSpec((None, 6, D), _mod_ix_ctx if need_ctx else _mod_ix_lat)],
        out_specs=pl.BlockSpec((None, TM, D), lambda b, i: (b, i, 0)),
        out_shape=jax.ShapeDtypeStruct((B, n * TM, D), F32),
        compiler_params=_cparams("parallel", "parallel"),
        name="out_proj",
    )(a, w, xs, mod)


def _ml_out_kernel(hf_ref, hb_ref, og_ref, ng_ref, w_ref, x_ref, mod_ref, o_ref):
    parts = []
    for h in range(ML_HEADS):
        sl = slice(h * ML_V, (h + 1) * ML_V)
        y = _rms(hf_ref[:, sl] + hb_ref[:, sl], ng_ref[:, sl])
        parts.append((y * jax.nn.sigmoid(og_ref[:, sl])).astype(BF16))
    y = jnp.dot(jnp.concatenate(parts, axis=1), w_ref[...], preferred_element_type=F32)
    o_ref[...] = x_ref[...] + mod_ref[2:3, :] * y


def _ml_out(hf, hb, og, ng, w, xs, mod):
    tile = pl.BlockSpec((None, TM, D), lambda b, i: (b, i, 0))
    return pl.pallas_call(
        _ml_out_kernel,
        grid=(B, NT),
        in_specs=[tile, tile, tile, _resident((1, D)), _resident((D, D)), tile,
                  pl.BlockSpec((None, 6, D), _mod_ix_ctx)],
        out_specs=tile,
        out_shape=jax.ShapeDtypeStruct((B, T, D), F32),
        compiler_params=_cparams("parallel", "parallel"),
        name="ml_out",
    )(hf, hb, og, ng, w, xs, mod)


def _ffn_kernel(x_ref, xp_ref, xn_ref, mod_ref, g_ref, wup_ref, cw_ref, cb_ref, wdn_ref, fg_ref,
                o_ref, h_scr, ua_scr, ug_scr, act_scr, *, has_ctx, nt, final):
    i = pl.program_id(1)
    g = g_ref[...]
    shift, scale, gate = mod_ref[3:4, :], mod_ref[4:5, :], mod_ref[5:6, :]
    if has_ctx:
        prev_ok = i >= 2
        next_ok = (i != 0) & (i != nt - 1)
    else:
        prev_ok = i >= 1
        next_ok = i != nt - 1
    nb = TM // 8
    x = jnp.swapaxes(x_ref[...].reshape(8, nb, D), 0, 1).reshape(TM, D)
    h_scr[0:TM, :] = _norm_mod(x, g, shift, scale)
    sub = lax.broadcasted_iota(jnp.int32, (FFN_EDGE, D), 0)
    h_prev = jnp.where(prev_ok, _norm_mod(xp_ref[HALO - 1:HALO, :], g, shift, scale), 0.0)
    h_next = jnp.where(next_ok, _norm_mod(xn_ref[0:1, :], g, shift, scale), 0.0)
    h_scr[TM:, :] = jnp.where(sub == 0, h_prev, jnp.where(sub == 1, h_next, 0.0))
    hb = h_scr[...].astype(BF16)

    sub8 = lax.broadcasted_iota(jnp.int32, (8, FFN_CHUNK), 0)

    def conv(scr, off):
        w = cw_ref[:, off:off + FFN_CHUNK]
        first = jnp.where(sub8 == 0, scr[TM:TM + 1, :], pltpu.roll(scr[TM - 8:TM, :], 1, 0))
        final_grp = jnp.where(sub8 == 7, scr[TM + 1:TM + 2, :], pltpu.roll(scr[0:8, :], 7, 0))
        prev = jnp.concatenate([first, scr[0:TM - 8, :]], axis=0)
        nxt = jnp.concatenate([scr[8:TM, :], final_grp], axis=0)
        return prev * w[0:1] + scr[0:TM, :] * w[1:2] + nxt * w[2:3] + cb_ref[:, off:off + FFN_CHUNK]

    for c in range(FFN // FFN_CHUNK):
        lo = c * FFN_CHUNK
        ua_scr[c % 2] = jnp.dot(hb, wup_ref[:, lo:lo + FFN_CHUNK], preferred_element_type=F32)
        ug_scr[c % 2] = jnp.dot(hb, wup_ref[:, FFN + lo:FFN + lo + FFN_CHUNK], preferred_element_type=F32)
        a = conv(ua_scr.at[c % 2], lo)
        gg = conv(ug_scr.at[c % 2], FFN + lo)
        act_scr[:, lo:lo + FFN_CHUNK] = (a * (gg * jax.nn.sigmoid(gg))).astype(BF16)
    y = jnp.dot(act_scr[...], wdn_ref[...], preferred_element_type=F32)
    out = x + gate * y
    if final:
        out = _rms(out, fg_ref[...])
    o_ref[...] = jnp.swapaxes(out.reshape(nb, 8, D), 0, 1).reshape(TM, D)


def _ffn(xs, mod, g, wup, cw, cb, wdn, fg, has_ctx, final):
    rows = xs.shape[1]
    nt = rows // TM
    per_tile = TM // HALO
    last_halo = rows // HALO - 1
    return pl.pallas_call(
        functools.partial(_ffn_kernel, has_ctx=has_ctx, nt=nt, final=final),
        grid=(B, nt),
        in_specs=[pl.BlockSpec((None, TM, D), lambda b, i: (b, i, 0)),
                  pl.BlockSpec((None, HALO, D), lambda b, i: (b, jnp.maximum(i * per_tile - 1, 0), 0)),
                  pl.BlockSpec((None, HALO, D), lambda b, i: (b, jnp.minimum((i + 1) * per_tile, last_halo), 0)),
                  pl.BlockSpec((None, 6, D), _mod_ix_ctx if has_ctx else _mod_ix_lat),
                  _resident((1, D)),
                  _resident((D, 2 * FFN)),
                  _resident((3, 2 * FFN)),
                  _resident((1, 2 * FFN)),
                  _resident((FFN, D)),
                  _resident((1, D))],
        out_specs=pl.BlockSpec((None, TM, D), lambda b, i: (b, i, 0)),
        out_shape=jax.ShapeDtypeStruct((B, rows, D), F32),
        scratch_shapes=[pltpu.VMEM((TM + FFN_EDGE, D), F32),
                        pltpu.VMEM((2, TM + FFN_EDGE, FFN_CHUNK), F32),
                        pltpu.VMEM((2, TM + FFN_EDGE, FFN_CHUNK), F32),
                        pltpu.VMEM((TM, FFN), BF16)],
        compiler_params=_cparams("parallel", "parallel"),
        name="ffn",
    )(xs, xs, xs, mod, g, wup, cw, cb, wdn, fg)


def _rope_tables():
    rows = S // GRID_W
    quarter = HEAD_DIM // 4
    row = jnp.repeat(jnp.arange(rows, dtype=F32), GRID_W)
    col = jnp.tile(jnp.arange(GRID_W, dtype=F32), rows)
    inv = 10000.0 ** (-jnp.arange(quarter, dtype=F32) / quarter)
    ang_r = row[:, None] * inv
    ang_c = col[:, None] * inv
    cr, sr, cc, sc = jnp.cos(ang_r), jnp.sin(ang_r), jnp.cos(ang_c), jnp.sin(ang_c)
    z = jnp.zeros_like(sr)
    cos = jnp.concatenate([cr, cr, cc, cc], axis=1)
    sa = jnp.concatenate([-sr, z, -sc, z], axis=1)
    sb = jnp.concatenate([z, sr, z, sc], axis=1)
    pad = lambda t, v: jnp.concatenate([jnp.full((CTX, 128), v, F32), jnp.tile(t, (1, 2))], axis=0)
    return pad(cos, 1.0), pad(sa, 0.0), pad(sb, 0.0)


def _group_major(w, axis):
    shape = w.shape
    w = w.reshape(shape[:axis] + (SW_KV, SW_GRP, HEAD_DIM) + shape[axis + 1:])
    w = jnp.swapaxes(w, axis, axis + 1)
    return w.reshape(shape)


def _row(v):
    return v.reshape(1, -1)


def _bf(w):
    return w.astype(BF16)


def _mod(cpad, w, b):
    return _ada(cpad, w, b).reshape(MOD_ROWS, 6, D)


def _da_layer(idx, xs, mod, rope, norm1_g, w_in, lq1, lk1, lq2, lk2, subln_g, w_out):
    need_ctx = idx < DEPTH - 1
    lam_init = 0.8 - 0.6 * math.exp(-0.3 * idx)
    qkv = _attn_proj(xs, mod, _row(norm1_g), _bf(w_in), rope, 4, 4, 4)
    o = _da_attn(qkv, jnp.stack([lq1, lk1, lq2, lk2]), _row(subln_g), lam_init, need_ctx)
    return _out_proj(o, _bf(w_out), xs, mod, need_ctx)


def _ml_layer(xs, mod, norm1_g, w_in, gate_b, norm_g, w_out):
    wg = _bf(w_in[:, 3072:])
    qk, v, og, gc, gr = _ml_proj(xs, mod, _row(norm1_g), _bf(w_in[:, :3072]), wg, wg.T,
                                 _row(gate_b), gate_b.reshape(-1, 1))
    hf, hb = _mlstm(qk, v, gc, gr)
    return _ml_out(hf, hb, og, _row(norm_g), _bf(w_out), xs, mod)


def _sw_layer(xs, mod, rope, norm1_g, w_in, sink, w_out):
    w_q = _group_major(w_in[:, :D], 1)
    qkv = _attn_proj(xs, mod, _row(norm1_g), _bf(jnp.concatenate([w_q, w_in[:, D:]], axis=1)), rope, 4, 1, 1)
    o = _swa_attn(qkv, _row(sink))
    return _out_proj(o, _bf(_group_major(w_out, 0)), xs, mod, True)


def _ffn_layer(idx, xs, mod, norm2_g, w_up, conv_w, conv_b, w_down, final_norm_g):
    final = idx == DEPTH - 1
    return _ffn(xs, mod, _row(norm2_g), _bf(w_up), conv_w, _row(conv_b), _bf(w_down),
                _row(final_norm_g), has_ctx=not final, final=final)


def kernel(x, c, ctx, c_ctx, l0_ada_w, l0_ada_b, l0_norm1_g, l0_da_w_in, l0_da_lam_q1, l0_da_lam_k1, l0_da_lam_q2, l0_da_lam_k2, l0_da_subln_g, l0_da_w_out, l0_norm2_g, l0_ffn_w_up, l0_ffn_conv_w, l0_ffn_conv_b, l0_ffn_w_down, l1_ada_w, l1_ada_b, l1_norm1_g, l1_ml_w_in, l1_ml_gate_b, l1_ml_norm_g, l1_ml_w_out, l1_norm2_g, l1_ffn_w_up, l1_ffn_conv_w, l1_ffn_conv_b, l1_ffn_w_down, l2_ada_w, l2_ada_b, l2_norm1_g, l2_sw_w_in, l2_sw_sink, l2_sw_w_out, l2_norm2_g, l2_ffn_w_up, l2_ffn_conv_w, l2_ffn_conv_b, l2_ffn_w_down, l3_ada_w, l3_ada_b, l3_norm1_g, l3_da_w_in, l3_da_lam_q1, l3_da_lam_k1, l3_da_lam_q2, l3_da_lam_k2, l3_da_subln_g, l3_da_w_out, l3_norm2_g, l3_ffn_w_up, l3_ffn_conv_w, l3_ffn_conv_b, l3_ffn_w_down, final_norm_g):
    rope = _rope_tables()
    xs = jnp.concatenate([ctx, x], axis=1)
    cpad = jnp.concatenate([c, c_ctx[None, :], jnp.zeros((MOD_ROWS - B - 1, D), F32)], axis=0)

    m = _mod(cpad, l0_ada_w, l0_ada_b)
    xs = _da_layer(0, xs, m, rope, l0_norm1_g, l0_da_w_in, l0_da_lam_q1, l0_da_lam_k1, l0_da_lam_q2, l0_da_lam_k2,
                   l0_da_subln_g, l0_da_w_out)
    xs = _ffn_layer(0, xs, m, l0_norm2_g, l0_ffn_w_up, l0_ffn_conv_w, l0_ffn_conv_b, l0_ffn_w_down, final_norm_g)

    m = _mod(cpad, l1_ada_w, l1_ada_b)
    xs = _ml_layer(xs, m, l1_norm1_g, l1_ml_w_in, l1_ml_gate_b, l1_ml_norm_g, l1_ml_w_out)
    xs = _ffn_layer(1, xs, m, l1_norm2_g, l1_ffn_w_up, l1_ffn_conv_w, l1_ffn_conv_b, l1_ffn_w_down, final_norm_g)

    m = _mod(cpad, l2_ada_w, l2_ada_b)
    xs = _sw_layer(xs, m, rope, l2_norm1_g, l2_sw_w_in, l2_sw_sink, l2_sw_w_out)
    xs = _ffn_layer(2, xs, m, l2_norm2_g, l2_ffn_w_up, l2_ffn_conv_w, l2_ffn_conv_b, l2_ffn_w_down, final_norm_g)

    m = _mod(cpad, l3_ada_w, l3_ada_b)
    xs = _da_layer(3, xs, m, rope, l3_norm1_g, l3_da_w_in, l3_da_lam_q1, l3_da_lam_k1, l3_da_lam_q2, l3_da_lam_k2,
                   l3_da_subln_g, l3_da_w_out)
    return _ffn_layer(3, xs, m, l3_norm2_g, l3_ffn_w_up, l3_ffn_conv_w, l3_ffn_conv_b, l3_ffn_w_down, final_norm_g)
```

```python
import functools
import math

import jax
import jax.numpy as jnp
from jax import lax
from jax.experimental import pallas as pl
from jax.experimental.pallas import tpu as pltpu

F32 = jnp.float32
BF16 = jnp.bfloat16
HIGHEST = lax.Precision.HIGHEST

D = 1024
B = 8
S = 2048
CTX = 256
T = CTX + S
DEPTH = 4
GRID_W = 64
HEAD_DIM = 64
EPS = 1e-6
NEG = -1e30

TM = 256
NT = T // TM
NS = S // TM
MOD_ROWS = 16
DA_HEADS = 8
DA_HPS = 8
DA_SLABS = 4
ML_HEADS = 8
ML_V = 128
SW_KV = 4
SW_GRP = 4
SW_WIN = 128
SW_QB = 128
FFN = 2816
FFN_CHUNK = 256
HALO = 8
FFN_EDGE = 16

LOG2E = math.log2(math.e)
Q_SCALE = HEAD_DIM ** -0.5 * LOG2E

NT_DIMS = (((1,), (1,)), ((), ()))
TN_DIMS = (((0,), (0,)), ((), ()))

VMEM_LIMIT = 56 * 1024 * 1024


def _cparams(*sem):
    return pltpu.CompilerParams(dimension_semantics=sem, vmem_limit_bytes=VMEM_LIMIT)


def _resident(shape):
    zeros = (0,) * len(shape)
    return pl.BlockSpec(shape, lambda *_: zeros, pipeline_mode=pl.Buffered(1))


def _rms(x, g):
    ms = jnp.mean(x * x, axis=-1, keepdims=True)
    return x * lax.rsqrt(ms + EPS) * g


def _norm_mod(x, g, shift, scale):
    return _rms(x, g) * (1.0 + scale) + shift


def _mod_ix_ctx(b, i):
    return (jnp.where(i == 0, B, b), 0, 0)


def _mod_ix_lat(b, i):
    return (b, 0, 0)


def _ada_kernel(c_ref, w_ref, b_ref, o_ref):
    c = c_ref[...]
    a = (c * jax.nn.sigmoid(c)).astype(BF16)
    o_ref[...] = jnp.dot(a, w_ref[...].astype(BF16), preferred_element_type=F32) + b_ref[...]


def _ada(cpad, w, bias):
    n = w.shape[1]
    tn = 1024
    return pl.pallas_call(
        _ada_kernel,
        grid=(n // tn,),
        in_specs=[pl.BlockSpec((MOD_ROWS, D), lambda j: (0, 0)),
                  pl.BlockSpec((D, tn), lambda j: (0, j)),
                  pl.BlockSpec((1, tn), lambda j: (0, j))],
        out_specs=pl.BlockSpec((MOD_ROWS, tn), lambda j: (0, j)),
        out_shape=jax.ShapeDtypeStruct((MOD_ROWS, n), F32),
        compiler_params=_cparams("arbitrary"),
        name="ada",
    )(cpad, w, bias.reshape(1, n))


def _rope(c, cos, sa, sb):
    return c * cos + pltpu.roll(c, 112, 1) * sa + pltpu.roll(c, 16, 1) * sb


def _attn_proj_kernel(x_ref, mod_ref, g_ref, w_ref, cos_ref, sa_ref, sb_ref, o_ref, *, nq, nk, nv):
    h = _norm_mod(x_ref[...], g_ref[...], mod_ref[0:1, :], mod_ref[1:2, :]).astype(BF16)
    cos, sa, sb = cos_ref[...], sa_ref[...], sb_ref[...]
    for j in range(nq + nk + nv):
        y = jnp.dot(h, w_ref[:, j * 256:(j + 1) * 256], preferred_element_type=F32)
        if j < nq + nk:
            halves = []
            for t in range(2):
                r = _rope(y[:, t * 128:(t + 1) * 128], cos, sa, sb)
                halves.append(r * Q_SCALE if j < nq else r)
            y = jnp.concatenate(halves, axis=1)
        o_ref[:, j * 256:(j + 1) * 256] = y.astype(BF16)


def _attn_proj(xs, mod, g, w, rope, nq, nk, nv):
    n = (nq + nk + nv) * 256
    cos, sa, sb = rope
    tab = pl.BlockSpec((TM, 128), lambda b, i: (i, 0))
    return pl.pallas_call(
        functools.partial(_attn_proj_kernel, nq=nq, nk=nk, nv=nv),
        grid=(B, NT),
        in_specs=[pl.BlockSpec((None, TM, D), lambda b, i: (b, i, 0)),
                  pl.BlockSpec((None, 6, D), _mod_ix_ctx),
                  _resident((1, D)),
                  _resident((D, n)),
                  tab, tab, tab],
        out_specs=pl.BlockSpec((None, TM, n), lambda b, i: (b, i, 0)),
        out_shape=jax.ShapeDtypeStruct((B, T, n), BF16),
        compiler_params=_cparams("parallel", "parallel"),
        name="attn_proj",
    )(xs, mod, g, w, cos, sa, sb)


def _ml_proj_kernel(x_ref, mod_ref, g_ref, w_ref, wg_ref, wgt_ref, gb_ref, gbt_ref,
                    qk_ref, vt_ref, o_ref, gc_ref, gr_ref):
    h = _norm_mod(x_ref[...], g_ref[...], mod_ref[0:1, :], mod_ref[1:2, :]).astype(BF16)
    for j in range(12):
        y = jnp.dot(h, w_ref[:, j * 256:(j + 1) * 256], preferred_element_type=F32)
        if j < 2:
            qk_ref[:, j * 256:(j + 1) * 256] = (y * 0.125).astype(BF16)
        elif j < 4:
            qk_ref[:, j * 256:(j + 1) * 256] = y.astype(BF16)
        elif j < 8:
            vt_ref[(j - 4) * 256:(j - 3) * 256, :] = y.T.astype(BF16)
        else:
            o_ref[:, (j - 8) * 256:(j - 7) * 256] = y
    gc_ref[...] = jnp.dot(h, wg_ref[...], preferred_element_type=F32) + gb_ref[...]
    gr_ref[...] = lax.dot_general(wgt_ref[...], h, NT_DIMS, preferred_element_type=F32) + gbt_ref[...]


def _ml_proj(xs, mod, g, w, wg, wgt, gb, gbt):
    tile = lambda n, dt: (pl.BlockSpec((None, TM, n), lambda b, i: (b, i, 0)),
                          jax.ShapeDtypeStruct((B, T, n), dt))
    tile_t = lambda n, dt: (pl.BlockSpec((None, n, TM), lambda b, i: (b, 0, i)),
                            jax.ShapeDtypeStruct((B, n, T), dt))
    outs = [tile(D, BF16), tile_t(D, BF16), tile(D, F32), tile(32, F32), tile_t(32, F32)]
    return pl.pallas_call(
        _ml_proj_kernel,
        grid=(B, NT),
        in_specs=[pl.BlockSpec((None, TM, D), lambda b, i: (b, i, 0)),
                  pl.BlockSpec((None, 6, D), _mod_ix_ctx),
                  _resident((1, D)),
                  _resident((D, 3072)),
                  _resident((D, 32)),
                  _resident((32, D)),
                  _resident((1, 32)),
                  _resident((32, 1))],
        out_specs=[o[0] for o in outs],
        out_shape=[o[1] for o in outs],
        compiler_params=_cparams("parallel", "parallel"),
        name="ml_proj",
    )(xs, mod, g, w, wg, wgt, gb, gbt)


def _da_attn_kernel(lam_ref, g_ref, q_ref, k_ref, v_ref, o_ref, s_scr, *, lam_init, has_ctx_tile):
    i = pl.program_id(2)
    lv = lam_ref[...]
    lam = (jnp.exp(jnp.sum(lv[0:1] * lv[1:2], axis=-1, keepdims=True))
           - jnp.exp(jnp.sum(lv[2:3] * lv[3:4], axis=-1, keepdims=True)) + lam_init)
    lane = lax.broadcasted_iota(jnp.int32, (TM, 128), 1)

    def attend(nk):
        def scores(hh):
            sl = slice(hh * 128, (hh + 1) * 128)
            q = q_ref[:, sl]
            for c in range(2):
                own = lane < HEAD_DIM if c == 0 else lane >= HEAD_DIM
                s_scr[(2 * hh + c) % DA_SLABS, 0:nk, :] = lax.dot_general(
                    k_ref[0:nk, sl], jnp.where(own, q, jnp.zeros_like(q)), NT_DIMS, preferred_element_type=F32)

        scores(0)
        for hh in range(DA_HPS):
            if hh + 1 < DA_HPS:
                scores(hh + 1)
            sl = slice(hh * 128, (hh + 1) * 128)
            es, ls = [], []
            for c in range(2):
                s = s_scr[(2 * hh + c) % DA_SLABS, 0:nk, :]
                e = jnp.exp2(s - jnp.max(s, axis=0, keepdims=True))
                es.append(e)
                ls.append(jnp.sum(e, axis=0, keepdims=True))
            wt = es[0] - es[1] * (lam * ls[0] / ls[1])
            ot = lax.dot_general(v_ref[0:nk, sl], wt.astype(BF16), TN_DIMS, preferred_element_type=F32)
            o = (ot * (1.0 / ls[0])).T
            o_ref[:, sl] = (_rms(o, g_ref[...]) * (1.0 - lam_init)).astype(BF16)

    if has_ctx_tile:
        pl.when(i == 0)(lambda: attend(CTX))
        pl.when(i > 0)(lambda: attend(T))
    else:
        attend(T)


def _da_attn(qkv, lamv, subln_g, lam_init, need_ctx):
    off = 0 if need_ctx else 1
    nq = NT - off
    ng = DA_HEADS // DA_HPS
    wl = DA_HPS * 128
    return pl.pallas_call(
        functools.partial(_da_attn_kernel, lam_init=lam_init, has_ctx_tile=need_ctx),
        grid=(B, ng, nq),
        in_specs=[_resident((4, HEAD_DIM)),
                  _resident((1, 128)),
                  pl.BlockSpec((None, TM, wl), lambda b, h, i: (b, i + off, h)),
                  pl.BlockSpec((None, T, wl), lambda b, h, i: (b, 0, ng + h)),
                  pl.BlockSpec((None, T, wl), lambda b, h, i: (b, 0, 2 * ng + h))],
        out_specs=pl.BlockSpec((None, TM, wl), lambda b, h, i: (b, i, h)),
        out_shape=jax.ShapeDtypeStruct((B, nq * TM, D), BF16),
        scratch_shapes=[pltpu.VMEM((DA_SLABS, T, TM), F32)],
        compiler_params=_cparams("parallel", "parallel", "parallel"),
        name="da_attn",
    )(lamv, subln_g, qkv, qkv, qkv)


def _swa_kernel(sink_ref, q_ref, k_ref, v_ref, o_ref, s_scr, acc_scr):
    blk = pl.program_id(1)
    nwin = 3 * SW_QB
    nq = SW_GRP * SW_QB
    grp = lax.broadcasted_iota(jnp.int32, (1, nq), 1) // SW_QB

    def body(is_lat):
        qs = jnp.concatenate([q_ref[:, g * 256:(g + 1) * 256] for g in range(SW_GRP)], axis=0)
        kc = k_ref[0:CTX, :]
        vc = v_ref[0:CTX, :]
        lane_c = lax.broadcasted_iota(jnp.int32, (CTX, 256), 1) // HEAD_DIM
        if is_lat:
            j = blk - CTX // SW_QB
            start = pl.multiple_of(jnp.minimum(CTX + (j - 1) * SW_QB, T - nwin), SW_QB)
            kw = k_ref[pl.ds(start, nwin), :]
            vw = v_ref[pl.ds(start, nwin), :]
            lane_w = lax.broadcasted_iota(jnp.int32, (nwin, 256), 1) // HEAD_DIM
            kpos = start - CTX + lax.broadcasted_iota(jnp.int32, (nwin, nq), 0)
            qpos = j * SW_QB + lax.broadcasted_iota(jnp.int32, (nwin, nq), 1) % SW_QB
            band = (jnp.abs(qpos - kpos) <= SW_WIN) & (kpos >= 0)
        nkeys = CTX + nwin if is_lat else CTX
        for h in range(SW_KV):
            kcz = jnp.where(lane_c == h, kc, jnp.zeros_like(kc))
            s_scr[h, 0:CTX, :] = lax.dot_general(kcz, qs, NT_DIMS, preferred_element_type=F32)
            if is_lat:
                kwz = jnp.where(lane_w == h, kw, jnp.zeros_like(kw))
                s_w = lax.dot_general(kwz, qs, NT_DIMS, preferred_element_type=F32)
                s_scr[h, CTX:nkeys, :] = jnp.where(band, s_w, NEG)
        vall = jnp.concatenate([vc, vw], axis=0) if is_lat else vc
        for h in range(SW_KV):
            sink = jnp.zeros((1, nq), F32)
            for g in range(SW_GRP):
                sg = sink_ref[0:1, SW_GRP * h + g:SW_GRP * h + g + 1] * LOG2E
                sink = jnp.where(grp == g, sg, sink)
            s = s_scr[h, 0:nkeys, :]
            m = jnp.maximum(jnp.max(s, axis=0, keepdims=True), sink)
            e = jnp.exp2(s - m)
            r = 1.0 / (jnp.sum(e, axis=0, keepdims=True) + jnp.exp2(sink - m))
            ot = lax.dot_general(vall, (e * r).astype(BF16), TN_DIMS, preferred_element_type=F32)
            rows = slice(h * HEAD_DIM, (h + 1) * HEAD_DIM)
            acc_scr[rows, :] = ot[rows, :]
        for g in range(SW_GRP):
            o_ref[:, g * 256:(g + 1) * 256] = acc_scr[:, g * SW_QB:(g + 1) * SW_QB].T.astype(BF16)

    pl.when(blk < CTX // SW_QB)(lambda: body(False))
    pl.when(blk >= CTX // SW_QB)(lambda: body(True))


def _swa_attn(qkv, sink):
    return pl.pallas_call(
        _swa_kernel,
        grid=(B, T // SW_QB),
        in_specs=[_resident((1, 16)),
                  pl.BlockSpec((None, SW_QB, D), lambda b, i: (b, i, 0)),
                  pl.BlockSpec((None, T, 256), lambda b, i: (b, 0, 4)),
                  pl.BlockSpec((None, T, 256), lambda b, i: (b, 0, 5))],
        out_specs=pl.BlockSpec((None, SW_QB, D), lambda b, i: (b, i, 0)),
        out_shape=jax.ShapeDtypeStruct((B, T, D), BF16),
        scratch_shapes=[pltpu.VMEM((SW_KV, CTX + 3 * SW_QB, SW_GRP * SW_QB), F32),
                        pltpu.VMEM((SW_KV * HEAD_DIM, SW_GRP * SW_QB), F32)],
        compiler_params=_cparams("parallel", "parallel"),
        name="swa_attn",
    )(sink, qkv, qkv, qkv)


def _logsig(x):
    return jnp.minimum(x, 0.0) - jnp.log(1.0 + jnp.exp(-jnp.abs(x)))


def _mlstm_kernel(qkf_ref, vtf_ref, gcf_ref, grf_ref, qkb_ref, vtb_ref, gcb_ref, grb_ref,
                  hf_ref, hb_ref, st_ref, m_ref):
    step = pl.program_id(1)

    @pl.when(step == 0)
    def _():
        st_ref[...] = jnp.zeros_like(st_ref)
        m_ref[...] = jnp.zeros_like(m_ref)

    L = TM
    row = lax.broadcasted_iota(jnp.int32, (L, L), 0)
    col = lax.broadcasted_iota(jnp.int32, (L, L), 1)
    lower = col <= row
    upper = col >= row
    lower_f = lower.astype(F32)
    upper_f = upper.astype(F32)
    lane = lax.broadcasted_iota(jnp.int32, (L, 128), 1)
    ones_t = jnp.where(lax.broadcasted_iota(jnp.int32, (ML_V, L), 0) == 0, 1.0, 0.0).astype(BF16)

    dirs = ((qkf_ref, vtf_ref, gcf_ref, grf_ref, hf_ref), (qkb_ref, vtb_ref, gcb_ref, grb_ref, hb_ref))
    for d, (qk_ref, vt_ref, gc_ref, gr_ref, h_ref) in enumerate(dirs):
        vis = upper if d == 0 else lower
        vis_f, vis_t_f = (upper_f, lower_f) if d == 0 else (lower_f, upper_f)
        last = L - 1 if d == 0 else 0
        gc = gc_ref[...]
        gr = gr_ref[...]
        lf_c = _logsig(gc[:, 16 * d + 8:16 * d + 16])
        lf_r = _logsig(gr[16 * d + 8:16 * d + 16, :])
        cum_r = jnp.dot(lf_r, vis_f, precision=HIGHEST, preferred_element_type=F32)
        cum_c = jnp.dot(vis_t_f, lf_c, precision=HIGHEST, preferred_element_type=F32)
        ci_all = gc[:, 16 * d:16 * d + 8] - cum_c
        for h in range(ML_HEADS):
            idx = d * ML_HEADS + h
            p = h // 2
            q2 = qk_ref[:, p * 128:(p + 1) * 128]
            k2 = qk_ref[:, 512 + p * 128:512 + (p + 1) * 128]
            own = lane >= HEAD_DIM if h % 2 else lane < HEAD_DIM
            qh = jnp.where(own, q2, jnp.zeros_like(q2))
            vt = vt_ref[h * ML_V:(h + 1) * ML_V, :]
            b_r = cum_r[h:h + 1, :]
            i_r = gr[16 * d + h:16 * d + h + 1, :]
            m_prev = m_ref[idx][0:1, 0:1]
            st = st_ref[idx]

            dm = jnp.where(vis, ci_all[:, h:h + 1] + b_r, NEG)
            a_r = b_r + m_prev
            m_r = jnp.maximum(a_r, jnp.max(dm, axis=0, keepdims=True))
            inter = jnp.exp(a_r - m_r)
            sc = lax.dot_general(k2, qh, NT_DIMS, preferred_element_type=F32) * jnp.exp(dm - m_r)
            carry = lax.dot_general(st.astype(BF16), qh, NT_DIMS, preferred_element_type=F32)
            num = inter * carry[0:ML_V, :] + jnp.dot(vt, sc.astype(BF16), preferred_element_type=F32)
            den = inter * carry[ML_V:ML_V + 1, :] + jnp.sum(sc, axis=0, keepdims=True)
            ht = num / jnp.maximum(jnp.abs(den), jnp.exp(-m_r))
            h_ref[:, h * ML_V:(h + 1) * ML_V] = ht.T

            m_new = m_r[:, last:last + 1]
            b_last = b_r[:, last:last + 1]
            g_r = jnp.exp(b_last - b_r + i_r - m_new)
            decay = jnp.exp(b_last + m_prev - m_new)
            gvt = (jnp.concatenate([vt, ones_t], axis=0).astype(F32) * g_r).astype(BF16)
            st_ref[idx] = decay * st + jnp.dot(gvt, k2, preferred_element_type=F32)
            m_ref[idx] = jnp.broadcast_to(m_new, (8, 128))


def _mlstm(qk, vt, gc, gr):
    fwd = lambda b, s: (b, s, 0)
    bwd = lambda b, s: (b, jnp.where(s == 0, 0, NT - s), 0)
    fwd_t = lambda b, s: (b, 0, s)
    bwd_t = lambda b, s: (b, 0, jnp.where(s == 0, 0, NT - s))
    tok = lambda n, ix: pl.BlockSpec((None, TM, n), ix)
    tok_t = lambda n, ix: pl.BlockSpec((None, n, TM), ix)
    return pl.pallas_call(
        _mlstm_kernel,
        grid=(B, NT),
        in_specs=[tok(D, fwd), tok_t(D, fwd_t), tok(32, fwd), tok_t(32, fwd_t),
                  tok(D, bwd), tok_t(D, bwd_t), tok(32, bwd), tok_t(32, bwd_t)],
        out_specs=[tok(D, fwd), tok(D, bwd)],
        out_shape=[jax.ShapeDtypeStruct((B, T, D), F32)] * 2,
        scratch_shapes=[pltpu.VMEM((2 * ML_HEADS, 2 * ML_V, 128), F32),
                        pltpu.VMEM((2 * ML_HEADS, 8, 128), F32)],
        compiler_params=_cparams("parallel", "arbitrary"),
        name="mlstm",
    )(qk, vt, gc, gr, qk, vt, gc, gr)


def _ml_gate_kernel(hf_ref, hb_ref, og_ref, ng_ref, a_ref):
    for h in range(ML_HEADS):
        sl = slice(h * ML_V, (h + 1) * ML_V)
        y = _rms(hf_ref[:, sl] + hb_ref[:, sl], ng_ref[:, sl])
        a_ref[:, sl] = (y * jax.nn.sigmoid(og_ref[:, sl])).astype(BF16)


def _ml_gate(hf, hb, og, ng):
    tile = pl.BlockSpec((None, TM, D), lambda b, i: (b, i, 0))
    return pl.pallas_call(
        _ml_gate_kernel,
        grid=(B, NT),
        in_specs=[tile, tile, tile, _resident((1, D))],
        out_specs=tile,
        out_shape=jax.ShapeDtypeStruct((B, T, D), BF16),
        compiler_params=_cparams("parallel", "parallel"),
        name="ml_gate",
    )(hf, hb, og, ng)


def _ffn_kernel(a_ref, ap_ref, an_ref, wo_ref, x_ref, xp_ref, xn_ref, mod_ref, g_ref, wup_ref, cw_ref, cb_ref,
                wdn_ref, fg_ref, o_ref, h_scr, ua_scr, ug_scr, act_scr, *, has_ctx, nt, final):
    i = pl.program_id(1)
    g = g_ref[...]
    shift, scale, gate = mod_ref[3:4, :], mod_ref[4:5, :], mod_ref[5:6, :]
    gate1 = mod_ref[2:3, :]
    sub = lax.broadcasted_iota(jnp.int32, (FFN_EDGE, D), 0)
    a_prev = ap_ref[...].astype(F32)[FFN_EDGE - 1:FFN_EDGE, :]
    a_next = an_ref[...].astype(F32)[0:1, :]
    a_edge = jnp.where(sub == 0, a_prev, jnp.where(sub == 1, a_next, 0.0)).astype(BF16)
    y1 = jnp.dot(jnp.concatenate([a_ref[...], a_edge], axis=0), wo_ref[...], preferred_element_type=F32)
    x1 = x_ref[...] + gate1 * y1[0:TM, :]
    x1_prev = xp_ref[HALO - 1:HALO, :] + gate1 * y1[TM:TM + 1, :]
    x1_next = xn_ref[0:1, :] + gate1 * y1[TM + 1:TM + 2, :]
    if has_ctx:
        prev_ok = i >= 2
        next_ok = (i != 0) & (i != nt - 1)
    else:
        prev_ok = i >= 1
        next_ok = i != nt - 1
    nb = TM // 8
    x = jnp.swapaxes(x1.reshape(8, nb, D), 0, 1).reshape(TM, D)
    h_scr[0:TM, :] = _norm_mod(x, g, shift, scale)
    h_prev = jnp.where(prev_ok, _norm_mod(x1_prev, g, shift, scale), 0.0)
    h_next = jnp.where(next_ok, _norm_mod(x1_next, g, shift, scale), 0.0)
    h_scr[TM:, :] = jnp.where(sub == 0, h_prev, jnp.where(sub == 1, h_next, 0.0))
    hb = h_scr[...].astype(BF16)

    sub8 = lax.broadcasted_iota(jnp.int32, (8, FFN_CHUNK), 0)

    def conv(scr, off):
        w = cw_ref[:, off:off + FFN_CHUNK]
        first = jnp.where(sub8 == 0, scr[TM:TM + 1, :], pltpu.roll(scr[TM - 8:TM, :], 1, 0))
        final_grp = jnp.where(sub8 == 7, scr[TM + 1:TM + 2, :], pltpu.roll(scr[0:8, :], 7, 0))
        prev = jnp.concatenate([first, scr[0:TM - 8, :]], axis=0)
        nxt = jnp.concatenate([scr[8:TM, :], final_grp], axis=0)
        return prev * w[0:1] + scr[0:TM, :] * w[1:2] + nxt * w[2:3] + cb_ref[:, off:off + FFN_CHUNK]

    for c in range(FFN // FFN_CHUNK):
        lo = c * FFN_CHUNK
        ua_scr[c % 2] = jnp.dot(hb, wup_ref[:, lo:lo + FFN_CHUNK], preferred_element_type=F32)
        ug_scr[c % 2] = jnp.dot(hb, wup_ref[:, FFN + lo:FFN + lo + FFN_CHUNK], preferred_element_type=F32)
        a = conv(ua_scr.at[c % 2], lo)
        gg = conv(ug_scr.at[c % 2], FFN + lo)
        act_scr[:, lo:lo + FFN_CHUNK] = (a * (gg * jax.nn.sigmoid(gg))).astype(BF16)
    y = jnp.dot(act_scr[...], wdn_ref[...], preferred_element_type=F32)
    out = x + gate * y
    if final:
        out = _rms(out, fg_ref[...])
    o_ref[...] = jnp.swapaxes(out.reshape(nb, 8, D), 0, 1).reshape(TM, D)


def _ffn(a, wo, xs, mod, g, wup, cw, cb, wdn, fg, has_ctx, final):
    rows = a.shape[1]
    nt = rows // TM
    off = 0 if has_ctx else 1
    x_per_tile = TM // HALO
    a_per_tile = TM // FFN_EDGE
    x_last = T // HALO - 1
    a_last = rows // FFN_EDGE - 1
    return pl.pallas_call(
        functools.partial(_ffn_kernel, has_ctx=has_ctx, nt=nt, final=final),
        grid=(B, nt),
        in_specs=[pl.BlockSpec((None, TM, D), lambda b, i: (b, i, 0)),
                  pl.BlockSpec((None, FFN_EDGE, D), lambda b, i: (b, jnp.maximum(i * a_per_tile - 1, 0), 0)),
                  pl.BlockSpec((None, FFN_EDGE, D), lambda b, i: (b, jnp.minimum((i + 1) * a_per_tile, a_last), 0)),
                  _resident((D, D)),
                  pl.BlockSpec((None, TM, D), lambda b, i: (b, i + off, 0)),
                  pl.BlockSpec((None, HALO, D), lambda b, i: (b, jnp.maximum((i + off) * x_per_tile - 1, 0), 0)),
                  pl.BlockSpec((None, HALO, D),
                               lambda b, i: (b, jnp.minimum((i + off + 1) * x_per_tile, x_last), 0)),
                  pl.BlockSpec((None, 6, D), _mod_ix_ctx if has_ctx else _mod_ix_lat),
                  _resident((1, D)),
                  _resident((D, 2 * FFN)),
                  _resident((3, 2 * FFN)),
                  _resident((1, 2 * FFN)),
                  _resident((FFN, D)),
                  _resident((1, D))],
        out_specs=pl.BlockSpec((None, TM, D), lambda b, i: (b, i, 0)),
        out_shape=jax.ShapeDtypeStruct((B, rows, D), F32),
        scratch_shapes=[pltpu.VMEM((TM + FFN_EDGE, D), F32),
                        pltpu.VMEM((2, TM + FFN_EDGE, FFN_CHUNK), F32),
                        pltpu.VMEM((2, TM + FFN_EDGE, FFN_CHUNK), F32),
                        pltpu.VMEM((TM, FFN), BF16)],
        compiler_params=_cparams("parallel", "parallel"),
        name="ffn",
    )(a, a, a, wo, xs, xs, xs, mod, g, wup, cw, cb, wdn, fg)


def _rope_tables():
    rows = S // GRID_W
    quarter = HEAD_DIM // 4
    row = jnp.repeat(jnp.arange(rows, dtype=F32), GRID_W)
    col = jnp.tile(jnp.arange(GRID_W, dtype=F32), rows)
    inv = 10000.0 ** (-jnp.arange(quarter, dtype=F32) / quarter)
    ang_r = row[:, None] * inv
    ang_c = col[:, None] * inv
    cr, sr, cc, sc = jnp.cos(ang_r), jnp.sin(ang_r), jnp.cos(ang_c), jnp.sin(ang_c)
    z = jnp.zeros_like(sr)
    cos = jnp.concatenate([cr, cr, cc, cc], axis=1)
    sa = jnp.concatenate([-sr, z, -sc, z], axis=1)
    sb = jnp.concatenate([z, sr, z, sc], axis=1)
    pad = lambda t, v: jnp.concatenate([jnp.full((CTX, 128), v, F32), jnp.tile(t, (1, 2))], axis=0)
    return pad(cos, 1.0), pad(sa, 0.0), pad(sb, 0.0)


def _group_major(w, axis):
    shape = w.shape
    w = w.reshape(shape[:axis] + (SW_KV, SW_GRP, HEAD_DIM) + shape[axis + 1:])
    w = jnp.swapaxes(w, axis, axis + 1)
    return w.reshape(shape)


def _row(v):
    return v.reshape(1, -1)


def _bf(w):
    return w.astype(BF16)


def _mod(cpad, w, b):
    return _ada(cpad, w, b).reshape(MOD_ROWS, 6, D)


def _da_layer(idx, xs, mod, rope, norm1_g, w_in, lq1, lk1, lq2, lk2, subln_g, w_out):
    need_ctx = idx < DEPTH - 1
    lam_init = 0.8 - 0.6 * math.exp(-0.3 * idx)
    qkv = _attn_proj(xs, mod, _row(norm1_g), _bf(w_in), rope, 4, 4, 4)
    o = _da_attn(qkv, jnp.stack([lq1, lk1, lq2, lk2]), _row(subln_g), lam_init, need_ctx)
    return o, _bf(w_out)


def _ml_layer(xs, mod, norm1_g, w_in, gate_b, norm_g, w_out):
    wg = _bf(w_in[:, 3072:])
    qk, v, og, gc, gr = _ml_proj(xs, mod, _row(norm1_g), _bf(w_in[:, :3072]), wg, wg.T,
                                 _row(gate_b), gate_b.reshape(-1, 1))
    hf, hb = _mlstm(qk, v, gc, gr)
    return _ml_gate(hf, hb, og, _row(norm_g)), _bf(w_out)


def _sw_layer(xs, mod, rope, norm1_g, w_in, sink, w_out):
    w_q = _group_major(w_in[:, :D], 1)
    qkv = _attn_proj(xs, mod, _row(norm1_g), _bf(jnp.concatenate([w_q, w_in[:, D:]], axis=1)), rope, 4, 1, 1)
    o = _swa_attn(qkv, _row(sink))
    return o, _bf(_group_major(w_out, 0))


def _ffn_layer(idx, mix, xs, mod, norm2_g, w_up, conv_w, conv_b, w_down, final_norm_g):
    final = idx == DEPTH - 1
    return _ffn(mix[0], mix[1], xs, mod, _row(norm2_g), _bf(w_up), conv_w, _row(conv_b), _bf(w_down),
                _row(final_norm_g), has_ctx=not final, final=final)


def kernel(x, c, ctx, c_ctx, l0_ada_w, l0_ada_b, l0_norm1_g, l0_da_w_in, l0_da_lam_q1, l0_da_lam_k1, l0_da_lam_q2, l0_da_lam_k2, l0_da_subln_g, l0_da_w_out, l0_norm2_g, l0_ffn_w_up, l0_ffn_conv_w, l0_ffn_conv_b, l0_ffn_w_down, l1_ada_w, l1_ada_b, l1_norm1_g, l1_ml_w_in, l1_ml_gate_b, l1_ml_norm_g, l1_ml_w_out, l1_norm2_g, l1_ffn_w_up, l1_ffn_conv_w, l1_ffn_conv_b, l1_ffn_w_down, l2_ada_w, l2_ada_b, l2_norm1_g, l2_sw_w_in, l2_sw_sink, l2_sw_w_out, l2_norm2_g, l2_ffn_w_up, l2_ffn_conv_w, l2_ffn_conv_b, l2_ffn_w_down, l3_ada_w, l3_ada_b, l3_norm1_g, l3_da_w_in, l3_da_lam_q1, l3_da_lam_k1, l3_da_lam_q2, l3_da_lam_k2, l3_da_subln_g, l3_da_w_out, l3_norm2_g, l3_ffn_w_up, l3_ffn_conv_w, l3_ffn_conv_b, l3_ffn_w_down, final_norm_g):
    rope = _rope_tables()
    xs = jnp.concatenate([ctx, x], axis=1)
    cpad = jnp.concatenate([c, c_ctx[None, :], jnp.zeros((MOD_ROWS - B - 1, D), F32)], axis=0)

    m = _mod(cpad, l0_ada_w, l0_ada_b)
    mix = _da_layer(0, xs, m, rope, l0_norm1_g, l0_da_w_in, l0_da_lam_q1, l0_da_lam_k1, l0_da_lam_q2, l0_da_lam_k2,
                    l0_da_subln_g, l0_da_w_out)
    xs = _ffn_layer(0, mix, xs, m, l0_norm2_g, l0_ffn_w_up, l0_ffn_conv_w, l0_ffn_conv_b, l0_ffn_w_down, final_norm_g)

    m = _mod(cpad, l1_ada_w, l1_ada_b)
    mix = _ml_layer(xs, m, l1_norm1_g, l1_ml_w_in, l1_ml_gate_b, l1_ml_norm_g, l1_ml_w_out)
    xs = _ffn_layer(1, mix, xs, m, l1_norm2_g, l1_ffn_w_up, l1_ffn_conv_w, l1_ffn_conv_b, l1_ffn_w_down, final_norm_g)

    m = _mod(cpad, l2_ada_w, l2_ada_b)
    mix = _sw_layer(xs, m, rope, l2_norm1_g, l2_sw_w_in, l2_sw_sink, l2_sw_w_out)
    xs = _ffn_layer(2, mix, xs, m, l2_norm2_g, l2_ffn_w_up, l2_ffn_conv_w, l2_ffn_conv_b, l2_ffn_w_down, final_norm_g)

    m = _mod(cpad, l3_ada_w, l3_ada_b)
    mix = _da_layer(3, xs, m, rope, l3_norm1_g, l3_da_w_in, l3_da_lam_q1, l3_da_lam_k1, l3_da_lam_q2, l3_da_lam_k2,
                    l3_da_subln_g, l3_da_w_out)
    return _ffn_layer(3, mix, xs, m, l3_norm2_g, l3_ffn_w_up, l3_ffn_conv_w, l3_ffn_conv_b, l3_ffn_w_down,
                      final_norm_g)
```

```python
import functools
import math

import jax
import jax.numpy as jnp
from jax import lax
from jax.experimental import pallas as pl
from jax.experimental.pallas import tpu as pltpu

F32 = jnp.float32
BF16 = jnp.bfloat16
HIGHEST = lax.Precision.HIGHEST

D = 1024
B = 8
S = 2048
CTX = 256
T = CTX + S
DEPTH = 4
GRID_W = 64
HEAD_DIM = 64
EPS = 1e-6
NEG = -1e30

TM = 256
NT = T // TM
NS = S // TM
MOD_ROWS = 16
DA_HEADS = 8
DA_HPS = 8
DA_SLABS = 6
DA_CHUNK = 256
ML_HEADS = 8
ML_V = 128
SW_KV = 4
SW_GRP = 4
SW_WIN = 128
SW_QB = 128
FFN = 2816
FFN_CHUNK = 256
HALO = 8
FFN_EDGE = 16
FFN_TPS = 2

LOG2E = math.log2(math.e)
Q_SCALE = HEAD_DIM ** -0.5 * LOG2E

NT_DIMS = (((1,), (1,)), ((), ()))
TN_DIMS = (((0,), (0,)), ((), ()))

VMEM_LIMIT = 56 * 1024 * 1024


def _cparams(*sem):
    return pltpu.CompilerParams(dimension_semantics=sem, vmem_limit_bytes=VMEM_LIMIT)


def _resident(shape):
    zeros = (0,) * len(shape)
    return pl.BlockSpec(shape, lambda *_: zeros, pipeline_mode=pl.Buffered(1))


def _rms(x, g):
    ms = jnp.mean(x * x, axis=-1, keepdims=True)
    return x * lax.rsqrt(ms + EPS) * g


def _norm_mod(x, g, shift, scale):
    return _rms(x, g) * (1.0 + scale) + shift


def _mod_ix_ctx(b, i):
    return (jnp.where(i == 0, B, b), 0, 0)


def _mod_ix_lat(b, i):
    return (b, 0, 0)


def _ada_kernel(c_ref, w_ref, b_ref, o_ref):
    c = c_ref[...]
    a = (c * jax.nn.sigmoid(c)).astype(BF16)
    o_ref[...] = jnp.dot(a, w_ref[...].astype(BF16), preferred_element_type=F32) + b_ref[...]


def _ada(cpad, w, bias):
    n = w.shape[1]
    tn = 1024
    return pl.pallas_call(
        _ada_kernel,
        grid=(n // tn,),
        in_specs=[pl.BlockSpec((MOD_ROWS, D), lambda j: (0, 0)),
                  pl.BlockSpec((D, tn), lambda j: (0, j)),
                  pl.BlockSpec((1, tn), lambda j: (0, j))],
        out_specs=pl.BlockSpec((MOD_ROWS, tn), lambda j: (0, j)),
        out_shape=jax.ShapeDtypeStruct((MOD_ROWS, n), F32),
        compiler_params=_cparams("arbitrary"),
        name="ada",
    )(cpad, w, bias.reshape(1, n))


def _rope(c, cos, sa, sb):
    return c * cos + pltpu.roll(c, 112, 1) * sa + pltpu.roll(c, 16, 1) * sb


def _attn_proj_kernel(x_ref, mod_ref, g_ref, w_ref, cos_ref, sa_ref, sb_ref, o_ref, *maybe_vt_ref, nq, nk, nv):
    h = _norm_mod(x_ref[...], g_ref[...], mod_ref[0:1, :], mod_ref[1:2, :]).astype(BF16)
    cos, sa, sb = cos_ref[...], sa_ref[...], sb_ref[...]
    for j in range(nq + nk + nv):
        y = jnp.dot(h, w_ref[:, j * 256:(j + 1) * 256], preferred_element_type=F32)
        if j < nq + nk:
            halves = []
            for t in range(2):
                r = _rope(y[:, t * 128:(t + 1) * 128], cos, sa, sb)
                halves.append(r * Q_SCALE if j < nq else r)
            y = jnp.concatenate(halves, axis=1)
        if j >= nq + nk and maybe_vt_ref:
            jv = j - nq - nk
            maybe_vt_ref[0][jv * 256:(jv + 1) * 256, :] = y.T.astype(BF16)
        else:
            o_ref[:, j * 256:(j + 1) * 256] = y.astype(BF16)


def _attn_proj(xs, mod, g, w, rope, nq, nk, nv, v_channel_major):
    n = (nq + nk + nv) * 256
    cos, sa, sb = rope
    tab = pl.BlockSpec((TM, 128), lambda b, i: (i, 0))
    if v_channel_major:
        n_tok = (nq + nk) * 256
        out_specs = [pl.BlockSpec((None, TM, n_tok), lambda b, i: (b, i, 0)),
                     pl.BlockSpec((None, nv * 256, TM), lambda b, i: (b, 0, i))]
        out_shape = [jax.ShapeDtypeStruct((B, T, n_tok), BF16), jax.ShapeDtypeStruct((B, nv * 256, T), BF16)]
    else:
        out_specs = pl.BlockSpec((None, TM, n), lambda b, i: (b, i, 0))
        out_shape = jax.ShapeDtypeStruct((B, T, n), BF16)
    return pl.pallas_call(
        functools.partial(_attn_proj_kernel, nq=nq, nk=nk, nv=nv),
        grid=(B, NT),
        in_specs=[pl.BlockSpec((None, TM, D), lambda b, i: (b, i, 0)),
                  pl.BlockSpec((None, 6, D), _mod_ix_ctx),
                  _resident((1, D)),
                  _resident((D, n)),
                  tab, tab, tab],
        out_specs=out_specs,
        out_shape=out_shape,
        compiler_params=_cparams("parallel", "parallel"),
        name="attn_proj",
    )(xs, mod, g, w, cos, sa, sb)


def _ml_proj_kernel(x_ref, mod_ref, g_ref, w_ref, wg_ref, wgt_ref, gb_ref, gbt_ref,
                    qk_ref, vt_ref, o_ref, gc_ref, gr_ref):
    h = _norm_mod(x_ref[...], g_ref[...], mod_ref[0:1, :], mod_ref[1:2, :]).astype(BF16)
    for j in range(12):
        y = jnp.dot(h, w_ref[:, j * 256:(j + 1) * 256], preferred_element_type=F32)
        if j < 2:
            qk_ref[:, j * 256:(j + 1) * 256] = (y * 0.125).astype(BF16)
        elif j < 4:
            qk_ref[:, j * 256:(j + 1) * 256] = y.astype(BF16)
        elif j < 8:
            vt_ref[(j - 4) * 256:(j - 3) * 256, :] = y.T.astype(BF16)
        else:
            o_ref[:, (j - 8) * 256:(j - 7) * 256] = y
    gc_ref[...] = jnp.dot(h, wg_ref[...], preferred_element_type=F32) + gb_ref[...]
    gr_ref[...] = lax.dot_general(wgt_ref[...], h, NT_DIMS, preferred_element_type=F32) + gbt_ref[...]


def _ml_proj(xs, mod, g, w, wg, wgt, gb, gbt):
    tile = lambda n, dt: (pl.BlockSpec((None, TM, n), lambda b, i: (b, i, 0)),
                          jax.ShapeDtypeStruct((B, T, n), dt))
    tile_t = lambda n, dt: (pl.BlockSpec((None, n, TM), lambda b, i: (b, 0, i)),
                            jax.ShapeDtypeStruct((B, n, T), dt))
    outs = [tile(D, BF16), tile_t(D, BF16), tile(D, F32), tile(32, F32), tile_t(32, F32)]
    return pl.pallas_call(
        _ml_proj_kernel,
        grid=(B, NT),
        in_specs=[pl.BlockSpec((None, TM, D), lambda b, i: (b, i, 0)),
                  pl.BlockSpec((None, 6, D), _mod_ix_ctx),
                  _resident((1, D)),
                  _resident((D, 3072)),
                  _resident((D, 32)),
                  _resident((32, D)),
                  _resident((1, 32)),
                  _resident((32, 1))],
        out_specs=[o[0] for o in outs],
        out_shape=[o[1] for o in outs],
        compiler_params=_cparams("parallel", "parallel"),
        name="ml_proj",
    )(xs, mod, g, w, wg, wgt, gb, gbt)


def _da_attn_kernel(lam_ref, g_ref, q_ref, k_ref, v_ref, o_ref, s_scr, w_scr, *, lam_init, has_ctx_tile):
    i = pl.program_id(2)
    lv = lam_ref[...]
    lam = (jnp.exp(jnp.sum(lv[0:1] * lv[1:2], axis=-1, keepdims=True))
           - jnp.exp(jnp.sum(lv[2:3] * lv[3:4], axis=-1, keepdims=True)) + lam_init)
    lane = lax.broadcasted_iota(jnp.int32, (TM, 128), 1)

    def attend(nk):
        nchunk = nk // DA_CHUNK
        heads = [dict() for _ in range(DA_HPS)]

        def slab(hh, c):
            return (hh % 3) * 2 + c

        def stage1(hh, j):
            st = heads[hh]
            sl = slice(hh * 128, (hh + 1) * 128)
            rows = slice(j * DA_CHUNK, (j + 1) * DA_CHUNK)
            if j == 0:
                q = q_ref[:, sl]
                st["q"] = [jnp.where(lane < HEAD_DIM, q, jnp.zeros_like(q)),
                           jnp.where(lane >= HEAD_DIM, q, jnp.zeros_like(q))]
                st["m"] = [None, None]
                st["l"] = [None, None]
            for c in range(2):
                s = lax.dot_general(k_ref[rows, sl], st["q"][c], NT_DIMS, preferred_element_type=F32)
                s_scr[slab(hh, c), rows, :] = s
                mx = jnp.max(s, axis=0, keepdims=True)
                st["m"][c] = mx if j == 0 else jnp.maximum(st["m"][c], mx)

        def stage2(hh, j):
            st = heads[hh]
            rows = slice(j * DA_CHUNK, (j + 1) * DA_CHUNK)
            for c in range(2):
                e = jnp.exp2(s_scr[slab(hh, c), rows, :] - st["m"][c])
                s_scr[slab(hh, c), rows, :] = e
                sm = jnp.sum(e, axis=0, keepdims=True)
                st["l"][c] = sm if j == 0 else st["l"][c] + sm

        def stage3(hh, j):
            st = heads[hh]
            rows = slice(j * DA_CHUNK, (j + 1) * DA_CHUNK)
            if j == 0:
                st["coef"] = lam * st["l"][0] / st["l"][1]
            w = s_scr[slab(hh, 0), rows, :] - s_scr[slab(hh, 1), rows, :] * st["coef"]
            w_scr[hh % 2, rows, :] = w.astype(BF16)
            if j == nchunk - 1:
                sl = slice(hh * 128, (hh + 1) * 128)
                ot = lax.dot_general(v_ref[0:nk, sl], w_scr[hh % 2, 0:nk, :], TN_DIMS, preferred_element_type=F32)
                o = (ot * (1.0 / st["l"][0])).T
                o_ref[:, sl] = (_rms(o, g_ref[...]) * (1.0 - lam_init)).astype(BF16)

        for t in range(-1, DA_HPS + 1):
            for j in range(nchunk):
                if 0 <= t + 1 < DA_HPS:
                    stage1(t + 1, j)
                if 0 <= t < DA_HPS:
                    stage2(t, j)
                if 0 <= t - 1 < DA_HPS:
                    stage3(t - 1, j)

    if has_ctx_tile:
        pl.when(i == 0)(lambda: attend(CTX))
        pl.when(i > 0)(lambda: attend(T))
    else:
        attend(T)


def _da_attn(qkv, lamv, subln_g, lam_init, need_ctx):
    off = 0 if need_ctx else 1
    nq = NT - off
    ng = DA_HEADS // DA_HPS
    wl = DA_HPS * 128
    return pl.pallas_call(
        functools.partial(_da_attn_kernel, lam_init=lam_init, has_ctx_tile=need_ctx),
        grid=(B, ng, nq),
        in_specs=[_resident((4, HEAD_DIM)),
                  _resident((1, 128)),
                  pl.BlockSpec((None, TM, wl), lambda b, h, i: (b, i + off, h)),
                  pl.BlockSpec((None, T, wl), lambda b, h, i: (b, 0, ng + h)),
                  pl.BlockSpec((None, T, wl), lambda b, h, i: (b, 0, 2 * ng + h))],
        out_specs=pl.BlockSpec((None, TM, wl), lambda b, h, i: (b, i, h)),
        out_shape=jax.ShapeDtypeStruct((B, nq * TM, D), BF16),
        scratch_shapes=[pltpu.VMEM((DA_SLABS, T, TM), F32),
                        pltpu.VMEM((2, T, TM), BF16)],
        compiler_params=_cparams("parallel", "parallel", "parallel"),
        name="da_attn",
    )(lamv, subln_g, qkv, qkv, qkv)


def _swa_kernel(sink_ref, q_ref, k_ref, v_ref, o_ref, s_scr, acc_scr):
    blk = pl.program_id(1)
    nwin = 3 * SW_QB
    nq = SW_GRP * SW_QB
    grp = lax.broadcasted_iota(jnp.int32, (1, nq), 1) // SW_QB

    def body(is_lat):
        qs = jnp.concatenate([q_ref[:, g * 256:(g + 1) * 256] for g in range(SW_GRP)], axis=0)
        kc = k_ref[0:CTX, :]
        vc = v_ref[0:CTX, :]
        lane_c = lax.broadcasted_iota(jnp.int32, (CTX, 256), 1) // HEAD_DIM
        if is_lat:
            j = blk - CTX // SW_QB
            start = pl.multiple_of(jnp.minimum(CTX + (j - 1) * SW_QB, T - nwin), SW_QB)
            kw = k_ref[pl.ds(start, nwin), :]
            vw = v_ref[pl.ds(start, nwin), :]
            lane_w = lax.broadcasted_iota(jnp.int32, (nwin, 256), 1) // HEAD_DIM
            kpos = start - CTX + lax.broadcasted_iota(jnp.int32, (nwin, nq), 0)
            qpos = j * SW_QB + lax.broadcasted_iota(jnp.int32, (nwin, nq), 1) % SW_QB
            band = (jnp.abs(qpos - kpos) <= SW_WIN) & (kpos >= 0)
        nkeys = CTX + nwin if is_lat else CTX
        for h in range(SW_KV):
            kcz = jnp.where(lane_c == h, kc, jnp.zeros_like(kc))
            s_scr[h, 0:CTX, :] = lax.dot_general(kcz, qs, NT_DIMS, preferred_element_type=F32)
            if is_lat:
                kwz = jnp.where(lane_w == h, kw, jnp.zeros_like(kw))
                s_w = lax.dot_general(kwz, qs, NT_DIMS, preferred_element_type=F32)
                s_scr[h, CTX:nkeys, :] = jnp.where(band, s_w, NEG)
        vall = jnp.concatenate([vc, vw], axis=0) if is_lat else vc
        for h in range(SW_KV):
            sink = jnp.zeros((1, nq), F32)
            for g in range(SW_GRP):
                sg = sink_ref[0:1, SW_GRP * h + g:SW_GRP * h + g + 1] * LOG2E
                sink = jnp.where(grp == g, sg, sink)
            s = s_scr[h, 0:nkeys, :]
            m = jnp.maximum(jnp.max(s, axis=0, keepdims=True), sink)
            e = jnp.exp2(s - m)
            r = 1.0 / (jnp.sum(e, axis=0, keepdims=True) + jnp.exp2(sink - m))
            ot = lax.dot_general(vall, (e * r).astype(BF16), TN_DIMS, preferred_element_type=F32)
            rows = slice(h * HEAD_DIM, (h + 1) * HEAD_DIM)
            acc_scr[rows, :] = ot[rows, :]
        for g in range(SW_GRP):
            o_ref[:, g * 256:(g + 1) * 256] = acc_scr[:, g * SW_QB:(g + 1) * SW_QB].T.astype(BF16)

    pl.when(blk < CTX // SW_QB)(lambda: body(False))
    pl.when(blk >= CTX // SW_QB)(lambda: body(True))


def _swa_attn(qkv, sink):
    return pl.pallas_call(
        _swa_kernel,
        grid=(B, T // SW_QB),
        in_specs=[_resident((1, 16)),
                  pl.BlockSpec((None, SW_QB, D), lambda b, i: (b, i, 0)),
                  pl.BlockSpec((None, T, 256), lambda b, i: (b, 0, 4)),
                  pl.BlockSpec((None, T, 256), lambda b, i: (b, 0, 5))],
        out_specs=pl.BlockSpec((None, SW_QB, D), lambda b, i: (b, i, 0)),
        out_shape=jax.ShapeDtypeStruct((B, T, D), BF16),
        scratch_shapes=[pltpu.VMEM((SW_KV, CTX + 3 * SW_QB, SW_GRP * SW_QB), F32),
                        pltpu.VMEM((SW_KV * HEAD_DIM, SW_GRP * SW_QB), F32)],
        compiler_params=_cparams("parallel", "parallel"),
        name="swa_attn",
    )(sink, qkv, qkv, qkv)


def _logsig(x):
    return jnp.minimum(x, 0.0) - jnp.log(1.0 + jnp.exp(-jnp.abs(x)))


def _mlstm_kernel(qkf_ref, vtf_ref, gcf_ref, grf_ref, qkb_ref, vtb_ref, gcb_ref, grb_ref,
                  hf_ref, hb_ref, st_ref, m_ref):
    step = pl.program_id(1)

    @pl.when(step == 0)
    def _():
        st_ref[...] = jnp.zeros_like(st_ref)
        m_ref[...] = jnp.zeros_like(m_ref)

    L = TM
    row = lax.broadcasted_iota(jnp.int32, (L, L), 0)
    col = lax.broadcasted_iota(jnp.int32, (L, L), 1)
    lower = col <= row
    upper = col >= row
    lower_f = lower.astype(F32)
    upper_f = upper.astype(F32)
    lane = lax.broadcasted_iota(jnp.int32, (L, 128), 1)
    ones_t = jnp.where(lax.broadcasted_iota(jnp.int32, (ML_V, L), 0) == 0, 1.0, 0.0).astype(BF16)

    dirs = ((qkf_ref, vtf_ref, gcf_ref, grf_ref, hf_ref), (qkb_ref, vtb_ref, gcb_ref, grb_ref, hb_ref))
    for d, (qk_ref, vt_ref, gc_ref, gr_ref, h_ref) in enumerate(dirs):
        vis = upper if d == 0 else lower
        vis_f, vis_t_f = (upper_f, lower_f) if d == 0 else (lower_f, upper_f)
        last = L - 1 if d == 0 else 0
        gc = gc_ref[...]
        gr = gr_ref[...]
        lf_c = _logsig(gc[:, 16 * d + 8:16 * d + 16])
        lf_r = _logsig(gr[16 * d + 8:16 * d + 16, :])
        cum_r = jnp.dot(lf_r, vis_f, precision=HIGHEST, preferred_element_type=F32)
        cum_c = jnp.dot(vis_t_f, lf_c, precision=HIGHEST, preferred_element_type=F32)
        ci_all = gc[:, 16 * d:16 * d + 8] - cum_c
        for h in range(ML_HEADS):
            idx = d * ML_HEADS + h
            p = h // 2
            q2 = qk_ref[:, p * 128:(p + 1) * 128]
            k2 = qk_ref[:, 512 + p * 128:512 + (p + 1) * 128]
            own = lane >= HEAD_DIM if h % 2 else lane < HEAD_DIM
            qh = jnp.where(own, q2, jnp.zeros_like(q2))
            vt = vt_ref[h * ML_V:(h + 1) * ML_V, :]
            b_r = cum_r[h:h + 1, :]
            i_r = gr[16 * d + h:16 * d + h + 1, :]
            m_prev = m_ref[idx][0:1, 0:1]
            st = st_ref[idx]

            dm = jnp.where(vis, ci_all[:, h:h + 1] + b_r, NEG)
            a_r = b_r + m_prev
            m_r = jnp.maximum(a_r, jnp.max(dm, axis=0, keepdims=True))
            inter = jnp.exp(a_r - m_r)
            sc = lax.dot_general(k2, qh, NT_DIMS, preferred_element_type=F32) * jnp.exp(dm - m_r)
            carry = lax.dot_general(st.astype(BF16), qh, NT_DIMS, preferred_element_type=F32)
            num = inter * carry[0:ML_V, :] + jnp.dot(vt, sc.astype(BF16), preferred_element_type=F32)
            den = inter * carry[ML_V:ML_V + 1, :] + jnp.sum(sc, axis=0, keepdims=True)
            ht = num / jnp.maximum(jnp.abs(den), jnp.exp(-m_r))
            h_ref[:, h * ML_V:(h + 1) * ML_V] = ht.T

            m_new = m_r[:, last:last + 1]
            b_last = b_r[:, last:last + 1]
            g_r = jnp.exp(b_last - b_r + i_r - m_new)
            decay = jnp.exp(b_last + m_prev - m_new)
            gvt = (jnp.concatenate([vt, ones_t], axis=0).astype(F32) * g_r).astype(BF16)
            st_ref[idx] = decay * st + jnp.dot(gvt, k2, preferred_element_type=F32)
            m_ref[idx] = jnp.broadcast_to(m_new, (8, 128))


def _mlstm(qk, vt, gc, gr):
    fwd = lambda b, s: (b, s, 0)
    bwd = lambda b, s: (b, jnp.where(s == 0, 0, NT - s), 0)
    fwd_t = lambda b, s: (b, 0, s)
    bwd_t = lambda b, s: (b, 0, jnp.where(s == 0, 0, NT - s))
    tok = lambda n, ix: pl.BlockSpec((None, TM, n), ix)
    tok_t = lambda n, ix: pl.BlockSpec((None, n, TM), ix)
    return pl.pallas_call(
        _mlstm_kernel,
        grid=(B, NT),
        in_specs=[tok(D, fwd), tok_t(D, fwd_t), tok(32, fwd), tok_t(32, fwd_t),
                  tok(D, bwd), tok_t(D, bwd_t), tok(32, bwd), tok_t(32, bwd_t)],
        out_specs=[tok(D, fwd), tok(D, bwd)],
        out_shape=[jax.ShapeDtypeStruct((B, T, D), F32)] * 2,
        scratch_shapes=[pltpu.VMEM((2 * ML_HEADS, 2 * ML_V, 128), F32),
                        pltpu.VMEM((2 * ML_HEADS, 8, 128), F32)],
        compiler_params=_cparams("parallel", "arbitrary"),
        name="mlstm",
    )(qk, vt, gc, gr, qk, vt, gc, gr)


def _ml_gate_kernel(hf_ref, hb_ref, og_ref, ng_ref, a_ref):
    for h in range(ML_HEADS):
        sl = slice(h * ML_V, (h + 1) * ML_V)
        y = _rms(hf_ref[:, sl] + hb_ref[:, sl], ng_ref[:, sl])
        a_ref[:, sl] = (y * jax.nn.sigmoid(og_ref[:, sl])).astype(BF16)


def _ml_gate(hf, hb, og, ng):
    tile = pl.BlockSpec((None, TM, D), lambda b, i: (b, i, 0))
    return pl.pallas_call(
        _ml_gate_kernel,
        grid=(B, NT),
        in_specs=[tile, tile, tile, _resident((1, D))],
        out_specs=tile,
        out_shape=jax.ShapeDtypeStruct((B, T, D), BF16),
        compiler_params=_cparams("parallel", "parallel"),
        name="ml_gate",
    )(hf, hb, og, ng)


def _ffn_kernel(*refs, has_ctx, nt, final):
    per_tile = 7
    shared = refs[FFN_TPS * per_tile:FFN_TPS * per_tile + 7]
    o_ref = refs[FFN_TPS * per_tile + 7]
    h_scr, x_scr, ua_scr, ug_scr, act_scr = refs[FFN_TPS * per_tile + 8:]
    tiles = []
    for k in range(FFN_TPS):
        tile = pl.program_id(0) * FFN_TPS + k
        tiles.append(_FfnTile(tile % nt, *refs[k * per_tile:(k + 1) * per_tile], *shared,
                              o_ref.at[k * TM:(k + 1) * TM, :], h_scr.at[k], x_scr.at[k], ua_scr.at[k],
                              ug_scr.at[k], act_scr.at[k], has_ctx=has_ctx, nt=nt, final=final))
    nchunk = FFN // FFN_CHUNK
    tiles[0].prologue()
    for k, t in enumerate(tiles):
        for c in range(nchunk):
            if c == nchunk // 2 and k + 1 < FFN_TPS:
                tiles[k + 1].prologue()
            t.chunk(c)
        t.finish()


class _FfnTile:
    def __init__(self, i, a_ref, ap_ref, an_ref, x_ref, xp_ref, xn_ref, mod_ref, wo_ref, g_ref, wup_ref, cw_ref,
                 cb_ref, wdn_ref, fg_ref, o_ref, h_scr, x_scr, ua_scr, ug_scr, act_scr, *, has_ctx, nt, final):
        self.__dict__.update(locals())

    def prologue(self):
        s = self
        mod_ref = s.mod_ref
        g = s.g_ref[...]
        shift, scale, gate1 = mod_ref[3:4, :], mod_ref[4:5, :], mod_ref[2:3, :]
        sub = lax.broadcasted_iota(jnp.int32, (FFN_EDGE, D), 0)
        a_prev = s.ap_ref[...].astype(F32)[FFN_EDGE - 1:FFN_EDGE, :]
        a_next = s.an_ref[...].astype(F32)[0:1, :]
        a_edge = jnp.where(sub == 0, a_prev, jnp.where(sub == 1, a_next, 0.0)).astype(BF16)
        y1 = jnp.dot(jnp.concatenate([s.a_ref[...], a_edge], axis=0), s.wo_ref[...], preferred_element_type=F32)
        x1 = s.x_ref[...] + gate1 * y1[0:TM, :]
        x1_prev = s.xp_ref[HALO - 1:HALO, :] + gate1 * y1[TM:TM + 1, :]
        x1_next = s.xn_ref[0:1, :] + gate1 * y1[TM + 1:TM + 2, :]
        if s.has_ctx:
            prev_ok = s.i >= 2
            next_ok = (s.i != 0) & (s.i != s.nt - 1)
        else:
            prev_ok = s.i >= 1
            next_ok = s.i != s.nt - 1
        x = jnp.swapaxes(x1.reshape(8, TM // 8, D), 0, 1).reshape(TM, D)
        s.x_scr[...] = x
        s.h_scr[0:TM, :] = _norm_mod(x, g, shift, scale)
        h_prev = jnp.where(prev_ok, _norm_mod(x1_prev, g, shift, scale), 0.0)
        h_next = jnp.where(next_ok, _norm_mod(x1_next, g, shift, scale), 0.0)
        s.h_scr[TM:, :] = jnp.where(sub == 0, h_prev, jnp.where(sub == 1, h_next, 0.0))
        s.hb = s.h_scr[...].astype(BF16)

    def _conv(self, scr, off):
        sub8 = lax.broadcasted_iota(jnp.int32, (8, FFN_CHUNK), 0)
        w = self.cw_ref[:, off:off + FFN_CHUNK]
        first = jnp.where(sub8 == 0, scr[TM:TM + 1, :], pltpu.roll(scr[TM - 8:TM, :], 1, 0))
        final_grp = jnp.where(sub8 == 7, scr[TM + 1:TM + 2, :], pltpu.roll(scr[0:8, :], 7, 0))
        prev = jnp.concatenate([first, scr[0:TM - 8, :]], axis=0)
        nxt = jnp.concatenate([scr[8:TM, :], final_grp], axis=0)
        return prev * w[0:1] + scr[0:TM, :] * w[1:2] + nxt * w[2:3] + self.cb_ref[:, off:off + FFN_CHUNK]

    def chunk(self, c):
        s = self
        lo = c * FFN_CHUNK
        s.ua_scr[c % 2] = jnp.dot(s.hb, s.wup_ref[:, lo:lo + FFN_CHUNK], preferred_element_type=F32)
        s.ug_scr[c % 2] = jnp.dot(s.hb, s.wup_ref[:, FFN + lo:FFN + lo + FFN_CHUNK], preferred_element_type=F32)
        a = s._conv(s.ua_scr.at[c % 2], lo)
        gg = s._conv(s.ug_scr.at[c % 2], FFN + lo)
        s.act_scr[:, lo:lo + FFN_CHUNK] = (a * (gg * jax.nn.sigmoid(gg))).astype(BF16)

    def finish(self):
        s = self
        y = jnp.dot(s.act_scr[...], s.wdn_ref[...], preferred_element_type=F32)
        out = s.x_scr[...] + s.mod_ref[5:6, :] * y
        if s.final:
            out = _rms(out, s.fg_ref[...])
        s.o_ref[...] = jnp.swapaxes(out.reshape(TM // 8, 8, D), 0, 1).reshape(TM, D)


def _ffn(a, wo, xs, mod, g, wup, cw, cb, wdn, fg, has_ctx, final):
    rows = a.shape[1]
    nt = rows // TM
    off = 0 if has_ctx else 1
    ntiles = B * nt
    a2 = a.reshape(B * rows, D)
    x2 = xs.reshape(B * T, D)
    x_per_tile = TM // HALO
    a_per_tile = TM // FFN_EDGE
    x_last = B * T // HALO - 1
    a_last = B * rows // FFN_EDGE - 1

    def tile_specs(k):
        ta = lambda j: j * FFN_TPS + k
        tx = lambda j: (ta(j) // nt) * NT + ta(j) % nt + off
        if has_ctx:
            mod_ix = lambda j: (jnp.where(ta(j) % nt == 0, B, ta(j) // nt), 0, 0)
        else:
            mod_ix = lambda j: (ta(j) // nt, 0, 0)
        return [pl.BlockSpec((TM, D), lambda j: (ta(j), 0)),
                pl.BlockSpec((FFN_EDGE, D), lambda j: (jnp.maximum(ta(j) * a_per_tile - 1, 0), 0)),
                pl.BlockSpec((FFN_EDGE, D), lambda j: (jnp.minimum((ta(j) + 1) * a_per_tile, a_last), 0)),
                pl.BlockSpec((TM, D), lambda j: (tx(j), 0)),
                pl.BlockSpec((HALO, D), lambda j: (jnp.maximum(tx(j) * x_per_tile - 1, 0), 0)),
                pl.BlockSpec((HALO, D), lambda j: (jnp.minimum((tx(j) + 1) * x_per_tile, x_last), 0)),
                pl.BlockSpec((None, 6, D), mod_ix)]

    per_tile_specs = [spec for k in range(FFN_TPS) for spec in tile_specs(k)]
    per_tile_args = [arr for _ in range(FFN_TPS) for arr in (a2, a2, a2, x2, x2, x2, mod)]
    out = pl.pallas_call(
        functools.partial(_ffn_kernel, has_ctx=has_ctx, nt=nt, final=final),
        grid=(ntiles // FFN_TPS,),
        in_specs=per_tile_specs + [_resident((D, D)),
                                   _resident((1, D)),
                                   _resident((D, 2 * FFN)),
                                   _resident((3, 2 * FFN)),
                                   _resident((1, 2 * FFN)),
                                   _resident((FFN, D)),
                                   _resident((1, D))],
        out_specs=pl.BlockSpec((FFN_TPS * TM, D), lambda j: (j, 0)),
        out_shape=jax.ShapeDtypeStruct((B * rows, D), F32),
        scratch_shapes=[pltpu.VMEM((FFN_TPS, TM + FFN_EDGE, D), F32),
                        pltpu.VMEM((FFN_TPS, TM, D), F32),
                        pltpu.VMEM((FFN_TPS, 2, TM + FFN_EDGE, FFN_CHUNK), F32),
                        pltpu.VMEM((FFN_TPS, 2, TM + FFN_EDGE, FFN_CHUNK), F32),
                        pltpu.VMEM((FFN_TPS, TM, FFN), BF16)],
        compiler_params=_cparams("parallel"),
        name="ffn",
    )(*per_tile_args, wo, g, wup, cw, cb, wdn, fg)
    return out.reshape(B, rows, D)


def _rope_tables():
    rows = S // GRID_W
    quarter = HEAD_DIM // 4
    row = jnp.repeat(jnp.arange(rows, dtype=F32), GRID_W)
    col = jnp.tile(jnp.arange(GRID_W, dtype=F32), rows)
    inv = 10000.0 ** (-jnp.arange(quarter, dtype=F32) / quarter)
    ang_r = row[:, None] * inv
    ang_c = col[:, None] * inv
    cr, sr, cc, sc = jnp.cos(ang_r), jnp.sin(ang_r), jnp.cos(ang_c), jnp.sin(ang_c)
    z = jnp.zeros_like(sr)
    cos = jnp.concatenate([cr, cr, cc, cc], axis=1)
    sa = jnp.concatenate([-sr, z, -sc, z], axis=1)
    sb = jnp.concatenate([z, sr, z, sc], axis=1)
    pad = lambda t, v: jnp.concatenate([jnp.full((CTX, 128), v, F32), jnp.tile(t, (1, 2))], axis=0)
    return pad(cos, 1.0), pad(sa, 0.0), pad(sb, 0.0)


def _group_major(w, axis):
    shape = w.shape
    w = w.reshape(shape[:axis] + (SW_KV, SW_GRP, HEAD_DIM) + shape[axis + 1:])
    w = jnp.swapaxes(w, axis, axis + 1)
    return w.reshape(shape)


def _row(v):
    return v.reshape(1, -1)


def _bf(w):
    return w.astype(BF16)


def _mod(cpad, w, b):
    return _ada(cpad, w, b).reshape(MOD_ROWS, 6, D)


def _da_layer(idx, xs, mod, rope, norm1_g, w_in, lq1, lk1, lq2, lk2, subln_g, w_out):
    need_ctx = idx < DEPTH - 1
    lam_init = 0.8 - 0.6 * math.exp(-0.3 * idx)
    qkv = _attn_proj(xs, mod, _row(norm1_g), _bf(w_in), rope, 4, 4, 4, v_channel_major=False)
    o = _da_attn(qkv, jnp.stack([lq1, lk1, lq2, lk2]), _row(subln_g), lam_init, need_ctx)
    return o, _bf(w_out)


def _ml_layer(xs, mod, norm1_g, w_in, gate_b, norm_g, w_out):
    wg = _bf(w_in[:, 3072:])
    qk, v, og, gc, gr = _ml_proj(xs, mod, _row(norm1_g), _bf(w_in[:, :3072]), wg, wg.T,
                                 _row(gate_b), gate_b.reshape(-1, 1))
    hf, hb = _mlstm(qk, v, gc, gr)
    return _ml_gate(hf, hb, og, _row(norm_g)), _bf(w_out)


def _sw_layer(xs, mod, rope, norm1_g, w_in, sink, w_out):
    w_q = _group_major(w_in[:, :D], 1)
    qkv = _attn_proj(xs, mod, _row(norm1_g), _bf(jnp.concatenate([w_q, w_in[:, D:]], axis=1)), rope, 4, 1, 1,
                     v_channel_major=False)
    o = _swa_attn(qkv, _row(sink))
    return o, _bf(_group_major(w_out, 0))


def _ffn_layer(idx, mix, xs, mod, norm2_g, w_up, conv_w, conv_b, w_down, final_norm_g):
    final = idx == DEPTH - 1
    return _ffn(mix[0], mix[1], xs, mod, _row(norm2_g), _bf(w_up), conv_w, _row(conv_b), _bf(w_down),
                _row(final_norm_g), has_ctx=not final, final=final)


def kernel(x, c, ctx, c_ctx, l0_ada_w, l0_ada_b, l0_norm1_g, l0_da_w_in, l0_da_lam_q1, l0_da_lam_k1, l0_da_lam_q2, l0_da_lam_k2, l0_da_subln_g, l0_da_w_out, l0_norm2_g, l0_ffn_w_up, l0_ffn_conv_w, l0_ffn_conv_b, l0_ffn_w_down, l1_ada_w, l1_ada_b, l1_norm1_g, l1_ml_w_in, l1_ml_gate_b, l1_ml_norm_g, l1_ml_w_out, l1_norm2_g, l1_ffn_w_up, l1_ffn_conv_w, l1_ffn_conv_b, l1_ffn_w_down, l2_ada_w, l2_ada_b, l2_norm1_g, l2_sw_w_in, l2_sw_sink, l2_sw_w_out, l2_norm2_g, l2_ffn_w_up, l2_ffn_conv_w, l2_ffn_conv_b, l2_ffn_w_down, l3_ada_w, l3_ada_b, l3_norm1_g, l3_da_w_in, l3_da_lam_q1, l3_da_lam_k1, l3_da_lam_q2, l3_da_lam_k2, l3_da_subln_g, l3_da_w_out, l3_norm2_g, l3_ffn_w_up, l3_ffn_conv_w, l3_ffn_conv_b, l3_ffn_w_down, final_norm_g):
    rope = _rope_tables()
    xs = jnp.concatenate([ctx, x], axis=1)
    cpad = jnp.concatenate([c, c_ctx[None, :], jnp.zeros((MOD_ROWS - B - 1, D), F32)], axis=0)

    m = _mod(cpad, l0_ada_w, l0_ada_b)
    mix = _da_layer(0, xs, m, rope, l0_norm1_g, l0_da_w_in, l0_da_lam_q1, l0_da_lam_k1, l0_da_lam_q2, l0_da_lam_k2,
                    l0_da_subln_g, l0_da_w_out)
    xs = _ffn_layer(0, mix, xs, m, l0_norm2_g, l0_ffn_w_up, l0_ffn_conv_w, l0_ffn_conv_b, l0_ffn_w_down, final_norm_g)

    m = _mod(cpad, l1_ada_w, l1_ada_b)
    mix = _ml_layer(xs, m, l1_norm1_g, l1_ml_w_in, l1_ml_gate_b, l1_ml_norm_g, l1_ml_w_out)
    xs = _ffn_layer(1, mix, xs, m, l1_norm2_g, l1_ffn_w_up, l1_ffn_conv_w, l1_ffn_conv_b, l1_ffn_w_down, final_norm_g)

    m = _mod(cpad, l2_ada_w, l2_ada_b)
    mix = _sw_layer(xs, m, rope, l2_norm1_g, l2_sw_w_in, l2_sw_sink, l2_sw_w_out)
    xs = _ffn_layer(2, mix, xs, m, l2_norm2_g, l2_ffn_w_up, l2_ffn_conv_w, l2_ffn_conv_b, l2_ffn_w_down, final_norm_g)

    m = _mod(cpad, l3_ada_w, l3_ada_b)
    mix = _da_layer(3, xs, m, rope, l3_norm1_g, l3_da_w_in, l3_da_lam_q1, l3_da_lam_k1, l3_da_lam_q2, l3_da_lam_k2,
                    l3_da_subln_g, l3_da_w_out)
    return _ffn_layer(3, mix, xs, m, l3_norm2_g, l3_ffn_w_up, l3_ffn_conv_w, l3_ffn_conv_b, l3_ffn_w_down,
                      final_norm_g)
```

```python
import functools
import math

import jax
import jax.numpy as jnp
from jax import lax
from jax.experimental import pallas as pl
from jax.experimental.pallas import tpu as pltpu

F32 = jnp.float32
BF16 = jnp.bfloat16

D = 1024
B = 8
S = 2048
CTX = 256
T = CTX + S
DEPTH = 4
GRID_W = 64
HEAD_DIM = 64
EPS = 1e-6
NEG = -1e30

TM = 256
NT = T // TM
NS = S // TM
MOD_ROWS = 16
DA_HEADS = 8
DA_HPS = 4
DA_SLABS = 6
DA_CHUNK = 256
ML_HEADS = 8
ML_V = 128
SW_KV = 4
SW_GRP = 4
SW_WIN = 128
SW_QB = 128
FFN = 2816
FFN_CHUNK = 256
HALO = 8
FFN_EDGE = 16
FFN_TPS = 2

LOG2E = math.log2(math.e)
Q_SCALE = HEAD_DIM ** -0.5 * LOG2E

NT_DIMS = (((1,), (1,)), ((), ()))
TN_DIMS = (((0,), (0,)), ((), ()))

VMEM_LIMIT = 56 * 1024 * 1024


def _cparams(*sem):
    return pltpu.CompilerParams(dimension_semantics=sem, vmem_limit_bytes=VMEM_LIMIT)


def _resident(shape):
    zeros = (0,) * len(shape)
    return pl.BlockSpec(shape, lambda *_: zeros, pipeline_mode=pl.Buffered(1))


def _rms(x, g):
    ms = jnp.mean(x * x, axis=-1, keepdims=True)
    return x * lax.rsqrt(ms + EPS) * g


def _norm_mod(x, g, shift, scale):
    return _rms(x, g) * (1.0 + scale) + shift


def _mod_ix_ctx(b, i):
    return (jnp.where(i == 0, B, b), 0, 0)


def _mod_ix_lat(b, i):
    return (b, 0, 0)


def _ada_kernel(c_ref, w_ref, b_ref, o_ref):
    c = c_ref[...]
    a = (c * jax.nn.sigmoid(c)).astype(BF16)
    o_ref[...] = jnp.dot(a, w_ref[...].astype(BF16), preferred_element_type=F32) + b_ref[...]


def _ada(cpad, w, bias):
    n = w.shape[1]
    tn = 1024
    return pl.pallas_call(
        _ada_kernel,
        grid=(n // tn,),
        in_specs=[pl.BlockSpec((MOD_ROWS, D), lambda j: (0, 0)),
                  pl.BlockSpec((D, tn), lambda j: (0, j)),
                  pl.BlockSpec((1, tn), lambda j: (0, j))],
        out_specs=pl.BlockSpec((MOD_ROWS, tn), lambda j: (0, j)),
        out_shape=jax.ShapeDtypeStruct((MOD_ROWS, n), F32),
        compiler_params=_cparams("arbitrary"),
        name="ada",
    )(cpad, w, bias.reshape(1, n))


def _rope(c, cos, sa, sb):
    return c * cos + pltpu.roll(c, 112, 1) * sa + pltpu.roll(c, 16, 1) * sb


def _attn_proj_kernel(x_ref, mod_ref, g_ref, w_ref, cos_ref, sa_ref, sb_ref, o_ref, *maybe_vt_ref, nq, nk, nv):
    h = _norm_mod(x_ref[...], g_ref[...], mod_ref[0:1, :], mod_ref[1:2, :]).astype(BF16)
    cos, sa, sb = cos_ref[...], sa_ref[...], sb_ref[...]
    for j in range(nq + nk + nv):
        y = jnp.dot(h, w_ref[:, j * 256:(j + 1) * 256], preferred_element_type=F32)
        if j < nq + nk:
            halves = []
            for t in range(2):
                r = _rope(y[:, t * 128:(t + 1) * 128], cos, sa, sb)
                halves.append(r * Q_SCALE if j < nq else r)
            y = jnp.concatenate(halves, axis=1)
        if j >= nq + nk and maybe_vt_ref:
            jv = j - nq - nk
            maybe_vt_ref[0][jv * 256:(jv + 1) * 256, :] = y.T.astype(BF16)
        else:
            o_ref[:, j * 256:(j + 1) * 256] = y.astype(BF16)


def _attn_proj(xs, mod, g, w, rope, nq, nk, nv, v_channel_major):
    n = (nq + nk + nv) * 256
    cos, sa, sb = rope
    tab = pl.BlockSpec((TM, 128), lambda b, i: (i, 0))
    if v_channel_major:
        n_tok = (nq + nk) * 256
        out_specs = [pl.BlockSpec((None, TM, n_tok), lambda b, i: (b, i, 0)),
                     pl.BlockSpec((None, nv * 256, TM), lambda b, i: (b, 0, i))]
        out_shape = [jax.ShapeDtypeStruct((B, T, n_tok), BF16), jax.ShapeDtypeStruct((B, nv * 256, T), BF16)]
    else:
        out_specs = pl.BlockSpec((None, TM, n), lambda b, i: (b, i, 0))
        out_shape = jax.ShapeDtypeStruct((B, T, n), BF16)
    return pl.pallas_call(
        functools.partial(_attn_proj_kernel, nq=nq, nk=nk, nv=nv),
        grid=(B, NT),
        in_specs=[pl.BlockSpec((None, TM, D), lambda b, i: (b, i, 0)),
                  pl.BlockSpec((None, 6, D), _mod_ix_ctx),
                  _resident((1, D)),
                  _resident((D, n)),
                  tab, tab, tab],
        out_specs=out_specs,
        out_shape=out_shape,
        compiler_params=_cparams("parallel", "parallel"),
        name="attn_proj",
    )(xs, mod, g, w, cos, sa, sb)


def _ml_proj_kernel(x_ref, mod_ref, g_ref, w_ref, wgt_ref, gbt_ref, qk_ref, vt_ref, o_ref, gr_ref):
    h = _norm_mod(x_ref[...], g_ref[...], mod_ref[0:1, :], mod_ref[1:2, :]).astype(BF16)
    for j in range(12):
        y = jnp.dot(h, w_ref[:, j * 256:(j + 1) * 256], preferred_element_type=F32)
        if j < 2:
            qk_ref[:, j * 256:(j + 1) * 256] = (y * 0.125).astype(BF16)
        elif j < 4:
            qk_ref[:, j * 256:(j + 1) * 256] = y.astype(BF16)
        elif j < 8:
            vt_ref[(j - 4) * 256:(j - 3) * 256, :] = y.T.astype(BF16)
        else:
            o_ref[:, (j - 8) * 256:(j - 7) * 256] = y
    gr_ref[...] = lax.dot_general(wgt_ref[...], h, NT_DIMS, preferred_element_type=F32) + gbt_ref[...]


def _ml_proj(xs, mod, g, w, wgt, gbt):
    tile = lambda n, dt: (pl.BlockSpec((None, TM, n), lambda b, i: (b, i, 0)),
                          jax.ShapeDtypeStruct((B, T, n), dt))
    tile_t = lambda n, dt: (pl.BlockSpec((None, n, TM), lambda b, i: (b, 0, i)),
                            jax.ShapeDtypeStruct((B, n, T), dt))
    outs = [tile(D, BF16), tile_t(D, BF16), tile(D, F32), tile_t(32, F32)]
    return pl.pallas_call(
        _ml_proj_kernel,
        grid=(B, NT),
        in_specs=[pl.BlockSpec((None, TM, D), lambda b, i: (b, i, 0)),
                  pl.BlockSpec((None, 6, D), _mod_ix_ctx),
                  _resident((1, D)),
                  _resident((D, 3072)),
                  _resident((32, D)),
                  _resident((32, 1))],
        out_specs=[o[0] for o in outs],
        out_shape=[o[1] for o in outs],
        compiler_params=_cparams("parallel", "parallel"),
        name="ml_proj",
    )(xs, mod, g, w, wgt, gbt)


def _da_attn_kernel(lam_ref, g_ref, q_ref, k_ref, v_ref, *rest, lam_init, nk):
    o_ref, s_scr, w_scr = rest[-3:]
    lv = lam_ref[...]
    lam = (jnp.exp(jnp.sum(lv[0:1] * lv[1:2], axis=-1, keepdims=True))
           - jnp.exp(jnp.sum(lv[2:3] * lv[3:4], axis=-1, keepdims=True)) + lam_init)
    lane = lax.broadcasted_iota(jnp.int32, (TM, 128), 1)

    def attend(nk):
        nchunk = nk // DA_CHUNK
        heads = [dict() for _ in range(DA_HPS)]

        def slab(hh, c):
            return (hh % 3) * 2 + c

        def stage1(hh, j):
            st = heads[hh]
            sl = slice(hh * 128, (hh + 1) * 128)
            rows = slice(j * DA_CHUNK, (j + 1) * DA_CHUNK)
            if j == 0:
                q = q_ref[:, sl]
                st["q"] = [jnp.where(lane < HEAD_DIM, q, jnp.zeros_like(q)),
                           jnp.where(lane >= HEAD_DIM, q, jnp.zeros_like(q))]
                st["m"] = [None, None]
                st["l"] = [None, None]
            for c in range(2):
                s = lax.dot_general(k_ref[rows, sl], st["q"][c], NT_DIMS, preferred_element_type=F32)
                s_scr[slab(hh, c), rows, :] = s
                mx = jnp.max(s, axis=0, keepdims=True)
                st["m"][c] = mx if j == 0 else jnp.maximum(st["m"][c], mx)

        def stage2(hh, j):
            st = heads[hh]
            rows = slice(j * DA_CHUNK, (j + 1) * DA_CHUNK)
            for c in range(2):
                e = jnp.exp2(s_scr[slab(hh, c), rows, :] - st["m"][c])
                s_scr[slab(hh, c), rows, :] = e
                sm = jnp.sum(e, axis=0, keepdims=True)
                st["l"][c] = sm if j == 0 else st["l"][c] + sm

        def stage3(hh, j):
            st = heads[hh]
            rows = slice(j * DA_CHUNK, (j + 1) * DA_CHUNK)
            if j == 0:
                st["coef"] = lam * st["l"][0] / st["l"][1]
            w = s_scr[slab(hh, 0), rows, :] - s_scr[slab(hh, 1), rows, :] * st["coef"]
            w_scr[hh % 2, rows, :] = w.astype(BF16)
            if j == nchunk - 1:
                sl = slice(hh * 128, (hh + 1) * 128)
                ot = lax.dot_general(v_ref[0:nk, sl], w_scr[hh % 2, 0:nk, :], TN_DIMS, preferred_element_type=F32)
                o = (ot * (1.0 / st["l"][0])).T
                o_ref[:, sl] = (_rms(o, g_ref[...]) * (1.0 - lam_init)).astype(BF16)

        for t in range(-1, DA_HPS + 1):
            for j in range(nchunk):
                if 0 <= t + 1 < DA_HPS:
                    stage1(t + 1, j)
                if 0 <= t < DA_HPS:
                    stage2(t, j)
                if 0 <= t - 1 < DA_HPS:
                    stage3(t - 1, j)

    attend(nk)


def _da_attn(qkv, lamv, subln_g, lam_init, need_ctx):
    ng = DA_HEADS // DA_HPS
    wl = DA_HPS * 128
    out_off = 1 if need_ctx else 0
    rows = (NS + out_off) * TM
    scratch = lambda nk: [pltpu.VMEM((DA_SLABS, nk, TM), F32), pltpu.VMEM((2, nk, TM), BF16)]
    small = [_resident((4, HEAD_DIM)), _resident((1, 128))]
    out = pl.pallas_call(
        functools.partial(_da_attn_kernel, lam_init=lam_init, nk=T),
        grid=(B, ng, NS),
        in_specs=small + [pl.BlockSpec((None, TM, wl), lambda b, h, i: (b, i + 1, h)),
                          pl.BlockSpec((None, T, wl), lambda b, h, i: (b, 0, ng + h)),
                          pl.BlockSpec((None, T, wl), lambda b, h, i: (b, 0, 2 * ng + h))],
        out_specs=pl.BlockSpec((None, TM, wl), lambda b, h, i: (b, i + out_off, h)),
        out_shape=jax.ShapeDtypeStruct((B, rows, D), BF16),
        scratch_shapes=scratch(T),
        compiler_params=_cparams("parallel", "parallel", "parallel"),
        name="da_attn",
    )(lamv, subln_g, qkv, qkv, qkv)
    if not need_ctx:
        return out
    return pl.pallas_call(
        functools.partial(_da_attn_kernel, lam_init=lam_init, nk=CTX),
        grid=(B, ng),
        in_specs=small + [pl.BlockSpec((None, TM, wl), lambda b, h: (b, 0, h)),
                          pl.BlockSpec((None, CTX, wl), lambda b, h: (b, 0, ng + h)),
                          pl.BlockSpec((None, CTX, wl), lambda b, h: (b, 0, 2 * ng + h)),
                          pl.BlockSpec(memory_space=pl.ANY)],
        out_specs=pl.BlockSpec((None, TM, wl), lambda b, h: (b, 0, h)),
        out_shape=jax.ShapeDtypeStruct((B, rows, D), BF16),
        input_output_aliases={5: 0},
        scratch_shapes=scratch(CTX),
        compiler_params=_cparams("parallel", "parallel"),
        name="da_attn_ctx",
    )(lamv, subln_g, qkv, qkv, qkv, out)


def _swa_kernel(sink_ref, q_ref, k_ref, v_ref, *rest, is_lat):
    o_ref, s_scr, acc_scr = rest[-3:]
    nwin = 3 * SW_QB
    nq = SW_GRP * SW_QB
    grp = lax.broadcasted_iota(jnp.int32, (1, nq), 1) // SW_QB

    def body():
        qs = jnp.concatenate([q_ref[:, g * 256:(g + 1) * 256] for g in range(SW_GRP)], axis=0)
        kc = k_ref[0:CTX, :]
        vc = v_ref[0:CTX, :]
        lane_c = lax.broadcasted_iota(jnp.int32, (CTX, 256), 1) // HEAD_DIM
        if is_lat:
            j = pl.program_id(1)
            start = pl.multiple_of(jnp.minimum(CTX + (j - 1) * SW_QB, T - nwin), SW_QB)
            kw = k_ref[pl.ds(start, nwin), :]
            vw = v_ref[pl.ds(start, nwin), :]
            lane_w = lax.broadcasted_iota(jnp.int32, (nwin, 256), 1) // HEAD_DIM
            kpos = start - CTX + lax.broadcasted_iota(jnp.int32, (nwin, nq), 0)
            qpos = j * SW_QB + lax.broadcasted_iota(jnp.int32, (nwin, nq), 1) % SW_QB
            band = (jnp.abs(qpos - kpos) <= SW_WIN) & (kpos >= 0)
        nkeys = CTX + nwin if is_lat else CTX
        for h in range(SW_KV):
            kcz = jnp.where(lane_c == h, kc, jnp.zeros_like(kc))
            s_scr[h, 0:CTX, :] = lax.dot_general(kcz, qs, NT_DIMS, preferred_element_type=F32)
            if is_lat:
                kwz = jnp.where(lane_w == h, kw, jnp.zeros_like(kw))
                s_w = lax.dot_general(kwz, qs, NT_DIMS, preferred_element_type=F32)
                s_scr[h, CTX:nkeys, :] = jnp.where(band, s_w, NEG)
        vall = jnp.concatenate([vc, vw], axis=0) if is_lat else vc
        for h in range(SW_KV):
            sink = jnp.zeros((1, nq), F32)
            for g in range(SW_GRP):
                sg = sink_ref[0:1, SW_GRP * h + g:SW_GRP * h + g + 1] * LOG2E
                sink = jnp.where(grp == g, sg, sink)
            s = s_scr[h, 0:nkeys, :]
            m = jnp.maximum(jnp.max(s, axis=0, keepdims=True), sink)
            e = jnp.exp2(s - m)
            r = 1.0 / (jnp.sum(e, axis=0, keepdims=True) + jnp.exp2(sink - m))
            rows = slice(h * HEAD_DIM, (h + 1) * HEAD_DIM)
            acc_scr[rows, :] = lax.dot_general(vall[:, rows], (e * r).astype(BF16), TN_DIMS,
                                               preferred_element_type=F32)
        for g in range(SW_GRP):
            o_ref[:, g * 256:(g + 1) * 256] = acc_scr[:, g * SW_QB:(g + 1) * SW_QB].T.astype(BF16)

    body()


def _swa_attn(qkv, sink):
    cblk = CTX // SW_QB
    scratch = lambda nkeys: [pltpu.VMEM((SW_KV, nkeys, SW_GRP * SW_QB), F32),
                             pltpu.VMEM((SW_KV * HEAD_DIM, SW_GRP * SW_QB), F32)]
    out = pl.pallas_call(
        functools.partial(_swa_kernel, is_lat=True),
        grid=(B, S // SW_QB),
        in_specs=[_resident((1, 16)),
                  pl.BlockSpec((None, SW_QB, D), lambda b, i: (b, i + cblk, 0)),
                  pl.BlockSpec((None, T, 256), lambda b, i: (b, 0, 4)),
                  pl.BlockSpec((None, T, 256), lambda b, i: (b, 0, 5))],
        out_specs=pl.BlockSpec((None, SW_QB, D), lambda b, i: (b, i + cblk, 0)),
        out_shape=jax.ShapeDtypeStruct((B, T, D), BF16),
        scratch_shapes=scratch(CTX + 3 * SW_QB),
        compiler_params=_cparams("parallel", "parallel"),
        name="swa_attn",
    )(sink, qkv, qkv, qkv)
    return pl.pallas_call(
        functools.partial(_swa_kernel, is_lat=False),
        grid=(B, cblk),
        in_specs=[_resident((1, 16)),
                  pl.BlockSpec((None, SW_QB, D), lambda b, i: (b, i, 0)),
                  pl.BlockSpec((None, CTX, 256), lambda b, i: (b, 0, 4)),
                  pl.BlockSpec((None, CTX, 256), lambda b, i: (b, 0, 5)),
                  pl.BlockSpec(memory_space=pl.ANY)],
        out_specs=pl.BlockSpec((None, SW_QB, D), lambda b, i: (b, i, 0)),
        out_shape=jax.ShapeDtypeStruct((B, T, D), BF16),
        input_output_aliases={4: 0},
        scratch_shapes=scratch(CTX),
        compiler_params=_cparams("parallel", "parallel"),
        name="swa_attn_ctx",
    )(sink, qkv, qkv, qkv, out)


def _logsig(x):
    return jnp.minimum(x, 0.0) - jnp.log(1.0 + jnp.exp(-jnp.abs(x)))


def _lane_cumsum(x, reverse):
    n = x.shape[1]
    lane = lax.broadcasted_iota(jnp.int32, x.shape, 1)
    k = 1
    while k < n:
        if reverse:
            x = x + jnp.where(lane < n - k, pltpu.roll(x, n - k, 1), 0.0)
        else:
            x = x + jnp.where(lane >= k, pltpu.roll(x, k, 1), 0.0)
        k *= 2
    return x


def _mlstm_kernel(qkf_ref, vtf_ref, grf_ref, qkb_ref, vtb_ref, grb_ref, hf_ref, hb_ref, st_ref, m_ref):
    step = pl.program_id(1)

    @pl.when(step == 0)
    def _():
        st_ref[...] = jnp.zeros_like(st_ref)
        m_ref[...] = jnp.zeros_like(m_ref)

    L = TM
    row = lax.broadcasted_iota(jnp.int32, (L, L), 0)
    col = lax.broadcasted_iota(jnp.int32, (L, L), 1)
    lane = lax.broadcasted_iota(jnp.int32, (L, 128), 1)
    ones_t = jnp.where(lax.broadcasted_iota(jnp.int32, (ML_V, L), 0) == 0, 1.0, 0.0).astype(BF16)

    dirs = ((qkf_ref, vtf_ref, grf_ref, hf_ref), (qkb_ref, vtb_ref, grb_ref, hb_ref))
    for d, (qk_ref, vt_ref, gr_ref, h_ref) in enumerate(dirs):
        vis = col >= row if d == 0 else col <= row
        last = L - 1 if d == 0 else 0
        gr = gr_ref[...]
        cum_r = _lane_cumsum(_logsig(gr[16 * d + 8:16 * d + 16, :]), reverse=d == 1)
        ci_r = gr[16 * d:16 * d + 8, :] - cum_r
        ci_all = jnp.concatenate([ci_r, jnp.zeros((128 - ML_HEADS, L), F32)], axis=0).T
        for h in range(ML_HEADS):
            idx = d * ML_HEADS + h
            p = h // 2
            q2 = qk_ref[:, p * 128:(p + 1) * 128]
            k2 = qk_ref[:, 512 + p * 128:512 + (p + 1) * 128]
            own = lane >= HEAD_DIM if h % 2 else lane < HEAD_DIM
            qh = jnp.where(own, q2, jnp.zeros_like(q2))
            vt = vt_ref[h * ML_V:(h + 1) * ML_V, :]
            b_r = cum_r[h:h + 1, :]
            i_r = gr[16 * d + h:16 * d + h + 1, :]
            m_prev = m_ref[idx][0:1, 0:1]
            st = st_ref[idx]

            dm = jnp.where(vis, ci_all[:, h:h + 1] + b_r, NEG)
            a_r = b_r + m_prev
            m_r = jnp.maximum(a_r, jnp.max(dm, axis=0, keepdims=True))
            inter = jnp.exp(a_r - m_r)
            sc = lax.dot_general(k2, qh, NT_DIMS, preferred_element_type=F32) * jnp.exp(dm - m_r)
            carry = lax.dot_general(st.astype(BF16), qh, NT_DIMS, preferred_element_type=F32)
            num = inter * carry[0:ML_V, :] + jnp.dot(vt, sc.astype(BF16), preferred_element_type=F32)
            den = inter * carry[ML_V:ML_V + 1, :] + jnp.sum(sc, axis=0, keepdims=True)
            ht = num / jnp.maximum(jnp.abs(den), jnp.exp(-m_r))
            h_ref[:, h * ML_V:(h + 1) * ML_V] = ht.T

            m_new = m_r[:, last:last + 1]
            b_last = b_r[:, last:last + 1]
            g_r = jnp.exp(b_last - b_r + i_r - m_new)
            decay = jnp.exp(b_last + m_prev - m_new)
            gvt = (jnp.concatenate([vt, ones_t], axis=0).astype(F32) * g_r).astype(BF16)
            st_ref[idx] = decay * st + jnp.dot(gvt, k2, preferred_element_type=F32)
            m_ref[idx] = jnp.broadcast_to(m_new, (8, 128))


def _mlstm(qk, vt, gr):
    fwd = lambda b, s: (b, s, 0)
    bwd = lambda b, s: (b, jnp.where(s == 0, 0, NT - s), 0)
    fwd_t = lambda b, s: (b, 0, s)
    bwd_t = lambda b, s: (b, 0, jnp.where(s == 0, 0, NT - s))
    tok = lambda n, ix: pl.BlockSpec((None, TM, n), ix)
    tok_t = lambda n, ix: pl.BlockSpec((None, n, TM), ix)
    return pl.pallas_call(
        _mlstm_kernel,
        grid=(B, NT),
        in_specs=[tok(D, fwd), tok_t(D, fwd_t), tok_t(32, fwd_t),
                  tok(D, bwd), tok_t(D, bwd_t), tok_t(32, bwd_t)],
        out_specs=[tok(D, fwd), tok(D, bwd)],
        out_shape=[jax.ShapeDtypeStruct((B, T, D), F32)] * 2,
        scratch_shapes=[pltpu.VMEM((2 * ML_HEADS, 2 * ML_V, 128), F32),
                        pltpu.VMEM((2 * ML_HEADS, 8, 128), F32)],
        compiler_params=_cparams("parallel", "arbitrary"),
        name="mlstm",
    )(qk, vt, gr, qk, vt, gr)


def _ml_gate_kernel(hf_ref, hb_ref, og_ref, ng_ref, a_ref):
    for h in range(ML_HEADS):
        sl = slice(h * ML_V, (h + 1) * ML_V)
        y = _rms(hf_ref[:, sl] + hb_ref[:, sl], ng_ref[:, sl])
        a_ref[:, sl] = (y * jax.nn.sigmoid(og_ref[:, sl])).astype(BF16)


def _ml_gate(hf, hb, og, ng):
    tile = pl.BlockSpec((None, TM, D), lambda b, i: (b, i, 0))
    return pl.pallas_call(
        _ml_gate_kernel,
        grid=(B, NT),
        in_specs=[tile, tile, tile, _resident((1, D))],
        out_specs=tile,
        out_shape=jax.ShapeDtypeStruct((B, T, D), BF16),
        compiler_params=_cparams("parallel", "parallel"),
        name="ml_gate",
    )(hf, hb, og, ng)


def _ffn_kernel(*refs, has_ctx, nt, final):
    per_tile = 7
    shared = refs[FFN_TPS * per_tile:FFN_TPS * per_tile + 7]
    o_ref = refs[FFN_TPS * per_tile + 7]
    h_scr, x_scr, ua_scr, ug_scr, act_scr = refs[FFN_TPS * per_tile + 8:]
    tiles = []
    for k in range(FFN_TPS):
        tile = pl.program_id(0) * FFN_TPS + k
        tiles.append(_FfnTile(tile % nt, *refs[k * per_tile:(k + 1) * per_tile], *shared,
                              o_ref.at[k * TM:(k + 1) * TM, :], h_scr.at[k], x_scr.at[k], ua_scr.at[k],
                              ug_scr.at[k], act_scr.at[k], has_ctx=has_ctx, nt=nt, final=final))
    nchunk = FFN // FFN_CHUNK
    tiles[0].prologue()
    for k, t in enumerate(tiles):
        for c in range(nchunk):
            if c == nchunk // 2 and k + 1 < FFN_TPS:
                tiles[k + 1].prologue()
            t.chunk(c)
        t.finish()


class _FfnTile:
    def __init__(self, i, a_ref, ap_ref, an_ref, x_ref, xp_ref, xn_ref, mod_ref, wo_ref, g_ref, wup_ref, cw_ref,
                 cb_ref, wdn_ref, fg_ref, o_ref, h_scr, x_scr, ua_scr, ug_scr, act_scr, *, has_ctx, nt, final):
        self.__dict__.update(locals())

    def prologue(self):
        s = self
        mod_ref = s.mod_ref
        g = s.g_ref[...]
        shift, scale, gate1 = mod_ref[3:4, :], mod_ref[4:5, :], mod_ref[2:3, :]
        sub = lax.broadcasted_iota(jnp.int32, (FFN_EDGE, D), 0)
        a_prev = s.ap_ref[...].astype(F32)[FFN_EDGE - 1:FFN_EDGE, :]
        a_next = s.an_ref[...].astype(F32)[0:1, :]
        a_edge = jnp.where(sub == 0, a_prev, jnp.where(sub == 1, a_next, 0.0)).astype(BF16)
        y1 = jnp.dot(jnp.concatenate([s.a_ref[...], a_edge], axis=0), s.wo_ref[...], preferred_element_type=F32)
        x1 = s.x_ref[...] + gate1 * y1[0:TM, :]
        x1_prev = s.xp_ref[HALO - 1:HALO, :] + gate1 * y1[TM:TM + 1, :]
        x1_next = s.xn_ref[0:1, :] + gate1 * y1[TM + 1:TM + 2, :]
        if s.has_ctx:
            prev_ok = s.i >= 2
            next_ok = (s.i != 0) & (s.i != s.nt - 1)
        else:
            prev_ok = s.i >= 1
            next_ok = s.i != s.nt - 1
        x = jnp.swapaxes(x1.reshape(8, TM // 8, D), 0, 1).reshape(TM, D)
        s.x_scr[...] = x
        s.h_scr[0:TM, :] = _norm_mod(x, g, shift, scale)
        h_prev = jnp.where(prev_ok, _norm_mod(x1_prev, g, shift, scale), 0.0)
        h_next = jnp.where(next_ok, _norm_mod(x1_next, g, shift, scale), 0.0)
        s.h_scr[TM:, :] = jnp.where(sub == 0, h_prev, jnp.where(sub == 1, h_next, 0.0))
        s.hb = s.h_scr[...].astype(BF16)

    def _conv(self, scr, off):
        sub8 = lax.broadcasted_iota(jnp.int32, (8, FFN_CHUNK), 0)
        w = self.cw_ref[:, off:off + FFN_CHUNK]
        first = jnp.where(sub8 == 0, scr[TM:TM + 1, :], pltpu.roll(scr[TM - 8:TM, :], 1, 0))
        final_grp = jnp.where(sub8 == 7, scr[TM + 1:TM + 2, :], pltpu.roll(scr[0:8, :], 7, 0))
        prev = jnp.concatenate([first, scr[0:TM - 8, :]], axis=0)
        nxt = jnp.concatenate([scr[8:TM, :], final_grp], axis=0)
        return prev * w[0:1] + scr[0:TM, :] * w[1:2] + nxt * w[2:3] + self.cb_ref[:, off:off + FFN_CHUNK]

    def chunk(self, c):
        s = self
        lo = c * FFN_CHUNK
        s.ua_scr[c % 2] = jnp.dot(s.hb, s.wup_ref[:, lo:lo + FFN_CHUNK], preferred_element_type=F32)
        s.ug_scr[c % 2] = jnp.dot(s.hb, s.wup_ref[:, FFN + lo:FFN + lo + FFN_CHUNK], preferred_element_type=F32)
        a = s._conv(s.ua_scr.at[c % 2], lo)
        gg = s._conv(s.ug_scr.at[c % 2], FFN + lo)
        s.act_scr[:, lo:lo + FFN_CHUNK] = (a * (gg * jax.nn.sigmoid(gg))).astype(BF16)

    def finish(self):
        s = self
        y = jnp.dot(s.act_scr[...], s.wdn_ref[...], preferred_element_type=F32)
        out = s.x_scr[...] + s.mod_ref[5:6, :] * y
        if s.final:
            out = _rms(out, s.fg_ref[...])
        s.o_ref[...] = jnp.swapaxes(out.reshape(TM // 8, 8, D), 0, 1).reshape(TM, D)


def _ffn(a, wo, xs, mod, g, wup, cw, cb, wdn, fg, has_ctx, final):
    rows = a.shape[1]
    nt = rows // TM
    off = 0 if has_ctx else 1
    ntiles = B * nt
    a2 = a.reshape(B * rows, D)
    x2 = xs.reshape(B * T, D)
    x_per_tile = TM // HALO
    a_per_tile = TM // FFN_EDGE
    x_last = B * T // HALO - 1
    a_last = B * rows // FFN_EDGE - 1

    def tile_specs(k):
        ta = lambda j: j * FFN_TPS + k
        tx = lambda j: (ta(j) // nt) * NT + ta(j) % nt + off
        if has_ctx:
            mod_ix = lambda j: (jnp.where(ta(j) % nt == 0, B, ta(j) // nt), 0, 0)
        else:
            mod_ix = lambda j: (ta(j) // nt, 0, 0)
        return [pl.BlockSpec((TM, D), lambda j: (ta(j), 0)),
                pl.BlockSpec((FFN_EDGE, D), lambda j: (jnp.maximum(ta(j) * a_per_tile - 1, 0), 0)),
                pl.BlockSpec((FFN_EDGE, D), lambda j: (jnp.minimum((ta(j) + 1) * a_per_tile, a_last), 0)),
                pl.BlockSpec((TM, D), lambda j: (tx(j), 0)),
                pl.BlockSpec((HALO, D), lambda j: (jnp.maximum(tx(j) * x_per_tile - 1, 0), 0)),
                pl.BlockSpec((HALO, D), lambda j: (jnp.minimum((tx(j) + 1) * x_per_tile, x_last), 0)),
                pl.BlockSpec((None, 6, D), mod_ix)]

    per_tile_specs = [spec for k in range(FFN_TPS) for spec in tile_specs(k)]
    per_tile_args = [arr for _ in range(FFN_TPS) for arr in (a2, a2, a2, x2, x2, x2, mod)]
    out = pl.pallas_call(
        functools.partial(_ffn_kernel, has_ctx=has_ctx, nt=nt, final=final),
        grid=(ntiles // FFN_TPS,),
        in_specs=per_tile_specs + [_resident((D, D)),
                                   _resident((1, D)),
                                   _resident((D, 2 * FFN)),
                                   _resident((3, 2 * FFN)),
                                   _resident((1, 2 * FFN)),
                                   _resident((FFN, D)),
                                   _resident((1, D))],
        out_specs=pl.BlockSpec((FFN_TPS * TM, D), lambda j: (j, 0)),
        out_shape=jax.ShapeDtypeStruct((B * rows, D), F32),
        scratch_shapes=[pltpu.VMEM((FFN_TPS, TM + FFN_EDGE, D), F32),
                        pltpu.VMEM((FFN_TPS, TM, D), F32),
                        pltpu.VMEM((FFN_TPS, 2, TM + FFN_EDGE, FFN_CHUNK), F32),
                        pltpu.VMEM((FFN_TPS, 2, TM + FFN_EDGE, FFN_CHUNK), F32),
                        pltpu.VMEM((FFN_TPS, TM, FFN), BF16)],
        compiler_params=_cparams("parallel"),
        name="ffn",
    )(*per_tile_args, wo, g, wup, cw, cb, wdn, fg)
    return out.reshape(B, rows, D)


def _rope_tables():
    rows = S // GRID_W
    quarter = HEAD_DIM // 4
    row = jnp.repeat(jnp.arange(rows, dtype=F32), GRID_W)
    col = jnp.tile(jnp.arange(GRID_W, dtype=F32), rows)
    inv = 10000.0 ** (-jnp.arange(quarter, dtype=F32) / quarter)
    ang_r = row[:, None] * inv
    ang_c = col[:, None] * inv
    cr, sr, cc, sc = jnp.cos(ang_r), jnp.sin(ang_r), jnp.cos(ang_c), jnp.sin(ang_c)
    z = jnp.zeros_like(sr)
    cos = jnp.concatenate([cr, cr, cc, cc], axis=1)
    sa = jnp.concatenate([-sr, z, -sc, z], axis=1)
    sb = jnp.concatenate([z, sr, z, sc], axis=1)
    pad = lambda t, v: jnp.concatenate([jnp.full((CTX, 128), v, F32), jnp.tile(t, (1, 2))], axis=0)
    return pad(cos, 1.0), pad(sa, 0.0), pad(sb, 0.0)


def _group_major(w, axis):
    shape = w.shape
    w = w.reshape(shape[:axis] + (SW_KV, SW_GRP, HEAD_DIM) + shape[axis + 1:])
    w = jnp.swapaxes(w, axis, axis + 1)
    return w.reshape(shape)


def _row(v):
    return v.reshape(1, -1)


def _bf(w):
    return w.astype(BF16)


def _mod(cpad, w, b):
    return _ada(cpad, w, b).reshape(MOD_ROWS, 6, D)


def _da_layer(idx, xs, mod, rope, norm1_g, w_in, lq1, lk1, lq2, lk2, subln_g, w_out):
    need_ctx = idx < DEPTH - 1
    lam_init = 0.8 - 0.6 * math.exp(-0.3 * idx)
    qkv = _attn_proj(xs, mod, _row(norm1_g), _bf(w_in), rope, 4, 4, 4, v_channel_major=False)
    o = _da_attn(qkv, jnp.stack([lq1, lk1, lq2, lk2]), _row(subln_g), lam_init, need_ctx)
    return o, _bf(w_out)


def _ml_layer(xs, mod, norm1_g, w_in, gate_b, norm_g, w_out):
    qk, vt, og, gr = _ml_proj(xs, mod, _row(norm1_g), _bf(w_in[:, :3072]), _bf(w_in[:, 3072:]).T,
                              gate_b.reshape(-1, 1))
    hf, hb = _mlstm(qk, vt, gr)
    return _ml_gate(hf, hb, og, _row(norm_g)), _bf(w_out)


def _sw_layer(xs, mod, rope, norm1_g, w_in, sink, w_out):
    w_q = _group_major(w_in[:, :D], 1)
    qkv = _attn_proj(xs, mod, _row(norm1_g), _bf(jnp.concatenate([w_q, w_in[:, D:]], axis=1)), rope, 4, 1, 1,
                     v_channel_major=False)
    o = _swa_attn(qkv, _row(sink))
    return o, _bf(_group_major(w_out, 0))


def _ffn_layer(idx, mix, xs, mod, norm2_g, w_up, conv_w, conv_b, w_down, final_norm_g):
    final = idx == DEPTH - 1
    return _ffn(mix[0], mix[1], xs, mod, _row(norm2_g), _bf(w_up), conv_w, _row(conv_b), _bf(w_down),
                _row(final_norm_g), has_ctx=not final, final=final)


def kernel(x, c, ctx, c_ctx, l0_ada_w, l0_ada_b, l0_norm1_g, l0_da_w_in, l0_da_lam_q1, l0_da_lam_k1, l0_da_lam_q2, l0_da_lam_k2, l0_da_subln_g, l0_da_w_out, l0_norm2_g, l0_ffn_w_up, l0_ffn_conv_w, l0_ffn_conv_b, l0_ffn_w_down, l1_ada_w, l1_ada_b, l1_norm1_g, l1_ml_w_in, l1_ml_gate_b, l1_ml_norm_g, l1_ml_w_out, l1_norm2_g, l1_ffn_w_up, l1_ffn_conv_w, l1_ffn_conv_b, l1_ffn_w_down, l2_ada_w, l2_ada_b, l2_norm1_g, l2_sw_w_in, l2_sw_sink, l2_sw_w_out, l2_norm2_g, l2_ffn_w_up, l2_ffn_conv_w, l2_ffn_conv_b, l2_ffn_w_down, l3_ada_w, l3_ada_b, l3_norm1_g, l3_da_w_in, l3_da_lam_q1, l3_da_lam_k1, l3_da_lam_q2, l3_da_lam_k2, l3_da_subln_g, l3_da_w_out, l3_norm2_g, l3_ffn_w_up, l3_ffn_conv_w, l3_ffn_conv_b, l3_ffn_w_down, final_norm_g):
    rope = _rope_tables()
    xs = jnp.concatenate([ctx, x], axis=1)
    cpad = jnp.concatenate([c, c_ctx[None, :], jnp.zeros((MOD_ROWS - B - 1, D), F32)], axis=0)

    m = _mod(cpad, l0_ada_w, l0_ada_b)
    mix = _da_layer(0, xs, m, rope, l0_norm1_g, l0_da_w_in, l0_da_lam_q1, l0_da_lam_k1, l0_da_lam_q2, l0_da_lam_k2,
                    l0_da_subln_g, l0_da_w_out)
    xs = _ffn_layer(0, mix, xs, m, l0_norm2_g, l0_ffn_w_up, l0_ffn_conv_w, l0_ffn_conv_b, l0_ffn_w_down, final_norm_g)

    m = _mod(cpad, l1_ada_w, l1_ada_b)
    mix = _ml_layer(xs, m, l1_norm1_g, l1_ml_w_in, l1_ml_gate_b, l1_ml_norm_g, l1_ml_w_out)
    xs = _ffn_layer(1, mix, xs, m, l1_norm2_g, l1_ffn_w_up, l1_ffn_conv_w, l1_ffn_conv_b, l1_ffn_w_down, final_norm_g)

    m = _mod(cpad, l2_ada_w, l2_ada_b)
    mix = _sw_layer(xs, m, rope, l2_norm1_g, l2_sw_w_in, l2_sw_sink, l2_sw_w_out)
    xs = _ffn_layer(2, mix, xs, m, l2_norm2_g, l2_ffn_w_up, l2_ffn_conv_w, l2_ffn_conv_b, l2_ffn_w_down, final_norm_g)

    m = _mod(cpad, l3_ada_w, l3_ada_b)
    mix = _da_layer(3, xs, m, rope, l3_norm1_g, l3_da_w_in, l3_da_lam_q1, l3_da_lam_k1, l3_da_lam_q2, l3_da_lam_k2,
                    l3_da_subln_g, l3_da_w_out)
    return _ffn_layer(3, mix, xs, m, l3_norm2_g, l3_ffn_w_up, l3_ffn_conv_w, l3_ffn_conv_b, l3_ffn_w_down,
                      final_norm_g)
```

```python
import functools
import math

import jax
import jax.numpy as jnp
from jax import lax
from jax.experimental import pallas as pl
from jax.experimental.pallas import tpu as pltpu

F32 = jnp.float32
BF16 = jnp.bfloat16

D = 1024
B = 8
S = 2048
CTX = 256
T = CTX + S
DEPTH = 4
GRID_W = 64
HEAD_DIM = 64
EPS = 1e-6
NEG = -1e30

TM = 256
NT = T // TM
NS = S // TM
MOD_ROWS = 16
DA_HEADS = 8
DA_HPS = 4
DA_SLABS = 6
DA_CHUNK = 256
ML_HEADS = 8
ML_V = 128
SW_KV = 4
SW_GRP = 4
SW_WIN = 128
SW_QB = 128
FFN = 2816
FFN_CHUNK = 256
HALO = 8
FFN_EDGE = 16
FFN_TPS = 2

LOG2E = math.log2(math.e)
Q_SCALE = HEAD_DIM ** -0.5 * LOG2E

NT_DIMS = (((1,), (1,)), ((), ()))
TN_DIMS = (((0,), (0,)), ((), ()))

VMEM_LIMIT = 56 * 1024 * 1024


def _cparams(*sem):
    return pltpu.CompilerParams(dimension_semantics=sem, vmem_limit_bytes=VMEM_LIMIT)


def _resident(shape):
    zeros = (0,) * len(shape)
    return pl.BlockSpec(shape, lambda *_: zeros, pipeline_mode=pl.Buffered(1))


def _rms(x, g):
    ms = jnp.mean(x * x, axis=-1, keepdims=True)
    return x * lax.rsqrt(ms + EPS) * g


def _norm_mod(x, g, shift, scale):
    return _rms(x, g) * (1.0 + scale) + shift


def _mod_ix_ctx(b, i):
    return (jnp.where(i == 0, B, b), 0, 0)


def _mod_ix_lat(b, i):
    return (b, 0, 0)


def _ada_kernel(c_ref, w_ref, b_ref, o_ref):
    c = c_ref[...]
    a = (c * jax.nn.sigmoid(c)).astype(BF16)
    o_ref[...] = jnp.dot(a, w_ref[...].astype(BF16), preferred_element_type=F32) + b_ref[...]


def _ada(cpad, w, bias):
    n = w.shape[1]
    tn = 1024
    return pl.pallas_call(
        _ada_kernel,
        grid=(n // tn,),
        in_specs=[pl.BlockSpec((MOD_ROWS, D), lambda j: (0, 0)),
                  pl.BlockSpec((D, tn), lambda j: (0, j)),
                  pl.BlockSpec((1, tn), lambda j: (0, j))],
        out_specs=pl.BlockSpec((MOD_ROWS, tn), lambda j: (0, j)),
        out_shape=jax.ShapeDtypeStruct((MOD_ROWS, n), F32),
        compiler_params=_cparams("arbitrary"),
        name="ada",
    )(cpad, w, bias.reshape(1, n))


def _rope(c, cos, sa, sb):
    return c * cos + pltpu.roll(c, 112, 1) * sa + pltpu.roll(c, 16, 1) * sb


def _attn_proj_kernel(x_ref, mod_ref, g_ref, w_ref, cos_ref, sa_ref, sb_ref, o_ref, *maybe_vt_ref, nq, nk, nv):
    h = _norm_mod(x_ref[...], g_ref[...], mod_ref[0:1, :], mod_ref[1:2, :]).astype(BF16)
    cos, sa, sb = cos_ref[...], sa_ref[...], sb_ref[...]
    for j in range(nq + nk + nv):
        y = jnp.dot(h, w_ref[:, j * 256:(j + 1) * 256], preferred_element_type=F32)
        if j < nq + nk:
            halves = []
            for t in range(2):
                r = _rope(y[:, t * 128:(t + 1) * 128], cos, sa, sb)
                halves.append(r * Q_SCALE if j < nq else r)
            y = jnp.concatenate(halves, axis=1)
        if j >= nq + nk and maybe_vt_ref:
            jv = j - nq - nk
            maybe_vt_ref[0][jv * 256:(jv + 1) * 256, :] = y.T.astype(BF16)
        else:
            o_ref[:, j * 256:(j + 1) * 256] = y.astype(BF16)


def _attn_proj(xs, mod, g, w, rope, nq, nk, nv, v_channel_major):
    n = (nq + nk + nv) * 256
    cos, sa, sb = rope
    tab = pl.BlockSpec((TM, 128), lambda b, i: (i, 0))
    if v_channel_major:
        n_tok = (nq + nk) * 256
        out_specs = [pl.BlockSpec((None, TM, n_tok), lambda b, i: (b, i, 0)),
                     pl.BlockSpec((None, nv * 256, TM), lambda b, i: (b, 0, i))]
        out_shape = [jax.ShapeDtypeStruct((B, T, n_tok), BF16), jax.ShapeDtypeStruct((B, nv * 256, T), BF16)]
    else:
        out_specs = pl.BlockSpec((None, TM, n), lambda b, i: (b, i, 0))
        out_shape = jax.ShapeDtypeStruct((B, T, n), BF16)
    return pl.pallas_call(
        functools.partial(_attn_proj_kernel, nq=nq, nk=nk, nv=nv),
        grid=(B, NT),
        in_specs=[pl.BlockSpec((None, TM, D), lambda b, i: (b, i, 0)),
                  pl.BlockSpec((None, 6, D), _mod_ix_ctx),
                  _resident((1, D)),
                  _resident((D, n)),
                  tab, tab, tab],
        out_specs=out_specs,
        out_shape=out_shape,
        compiler_params=_cparams("parallel", "parallel"),
        name="attn_proj",
    )(xs, mod, g, w, cos, sa, sb)


def _ml_proj_kernel(x_ref, mod_ref, g_ref, w_ref, wgt_ref, gbt_ref, qk_ref, vt_ref, o_ref, gr_ref):
    h = _norm_mod(x_ref[...], g_ref[...], mod_ref[0:1, :], mod_ref[1:2, :]).astype(BF16)
    for j in range(12):
        y = jnp.dot(h, w_ref[:, j * 256:(j + 1) * 256], preferred_element_type=F32)
        if j < 2:
            qk_ref[:, j * 256:(j + 1) * 256] = (y * 0.125).astype(BF16)
        elif j < 4:
            qk_ref[:, j * 256:(j + 1) * 256] = y.astype(BF16)
        elif j < 8:
            vt_ref[(j - 4) * 256:(j - 3) * 256, :] = y.T.astype(BF16)
        else:
            o_ref[:, (j - 8) * 256:(j - 7) * 256] = y
    gr_ref[...] = lax.dot_general(wgt_ref[...], h, NT_DIMS, preferred_element_type=F32) + gbt_ref[...]


def _ml_proj(xs, mod, g, w, wgt, gbt):
    tile = lambda n, dt: (pl.BlockSpec((None, TM, n), lambda b, i: (b, i, 0)),
                          jax.ShapeDtypeStruct((B, T, n), dt))
    tile_t = lambda n, dt: (pl.BlockSpec((None, n, TM), lambda b, i: (b, 0, i)),
                            jax.ShapeDtypeStruct((B, n, T), dt))
    outs = [tile(D, BF16), tile_t(D, BF16), tile(D, F32), tile_t(32, F32)]
    return pl.pallas_call(
        _ml_proj_kernel,
        grid=(B, NT),
        in_specs=[pl.BlockSpec((None, TM, D), lambda b, i: (b, i, 0)),
                  pl.BlockSpec((None, 6, D), _mod_ix_ctx),
                  _resident((1, D)),
                  _resident((D, 3072)),
                  _resident((32, D)),
                  _resident((32, 1))],
        out_specs=[o[0] for o in outs],
        out_shape=[o[1] for o in outs],
        compiler_params=_cparams("parallel", "parallel"),
        name="ml_proj",
    )(xs, mod, g, w, wgt, gbt)


def _da_attn_kernel(lam_ref, g_ref, q_ref, k_ref, v_ref, *rest, lam_init, nk, zero_first_tile=False):
    o_ref, s_scr, w_scr = rest[-3:]
    lv = lam_ref[...]
    lam = (jnp.exp(jnp.sum(lv[0:1] * lv[1:2], axis=-1, keepdims=True))
           - jnp.exp(jnp.sum(lv[2:3] * lv[3:4], axis=-1, keepdims=True)) + lam_init)
    lane = lax.broadcasted_iota(jnp.int32, (TM, 128), 1)

    def attend(nk):
        chunk = min(DA_CHUNK, nk)
        nchunk = nk // chunk
        heads = [dict() for _ in range(DA_HPS)]

        def slab(hh, c):
            return (hh % 3) * 2 + c

        def stage1(hh, j):
            st = heads[hh]
            sl = slice(hh * 128, (hh + 1) * 128)
            rows = slice(j * chunk, (j + 1) * chunk)
            if j == 0:
                q = q_ref[:, sl]
                st["q"] = [jnp.where(lane < HEAD_DIM, q, jnp.zeros_like(q)),
                           jnp.where(lane >= HEAD_DIM, q, jnp.zeros_like(q))]
                st["m"] = [None, None]
                st["l"] = [None, None]
            for c in range(2):
                s = lax.dot_general(k_ref[rows, sl], st["q"][c], NT_DIMS, preferred_element_type=F32)
                s_scr[slab(hh, c), rows, :] = s
                mx = jnp.max(s, axis=0, keepdims=True)
                st["m"][c] = mx if j == 0 else jnp.maximum(st["m"][c], mx)

        def stage2(hh, j):
            st = heads[hh]
            rows = slice(j * chunk, (j + 1) * chunk)
            for c in range(2):
                e = jnp.exp2(s_scr[slab(hh, c), rows, :] - st["m"][c])
                s_scr[slab(hh, c), rows, :] = e
                sm = jnp.sum(e, axis=0, keepdims=True)
                st["l"][c] = sm if j == 0 else st["l"][c] + sm

        def stage3(hh, j):
            st = heads[hh]
            rows = slice(j * chunk, (j + 1) * chunk)
            if j == 0:
                st["coef"] = lam * st["l"][0] / st["l"][1]
            w = s_scr[slab(hh, 0), rows, :] - s_scr[slab(hh, 1), rows, :] * st["coef"]
            w_scr[hh % 2, rows, :] = w.astype(BF16)
            if j == nchunk - 1:
                sl = slice(hh * 128, (hh + 1) * 128)
                ot = lax.dot_general(v_ref[0:nk, sl], w_scr[hh % 2, 0:nk, :], TN_DIMS, preferred_element_type=F32)
                o = (ot * (1.0 / st["l"][0])).T
                o_ref[:, sl] = (_rms(o, g_ref[...]) * (1.0 - lam_init)).astype(BF16)

        for t in range(-1, DA_HPS + 1):
            for j in range(nchunk):
                if 0 <= t + 1 < DA_HPS:
                    stage1(t + 1, j)
                if 0 <= t < DA_HPS:
                    stage2(t, j)
                if 0 <= t - 1 < DA_HPS:
                    stage3(t - 1, j)

    if zero_first_tile:
        @pl.when(pl.program_id(2) == 0)
        def _():
            o_ref[...] = jnp.zeros(o_ref.shape, o_ref.dtype)

        pl.when(pl.program_id(2) > 0)(lambda: attend(nk))
    else:
        attend(nk)


def _da_attn(qkv, lamv, subln_g, lam_init, need_ctx):
    ng = DA_HEADS // DA_HPS
    wl = DA_HPS * 128
    out_off = 1 if need_ctx else 0
    rows = (NS + out_off) * TM
    scratch = lambda nk: [pltpu.VMEM((DA_SLABS, nk, TM), F32), pltpu.VMEM((2, nk, TM), BF16)]
    small = [_resident((4, HEAD_DIM)), _resident((1, 128))]
    out = pl.pallas_call(
        functools.partial(_da_attn_kernel, lam_init=lam_init, nk=T, zero_first_tile=need_ctx),
        grid=(B, ng, NS + out_off),
        in_specs=small + [pl.BlockSpec((None, TM, wl), lambda b, h, i: (b, i + 1 - out_off, h)),
                          pl.BlockSpec((None, T, wl), lambda b, h, i: (b, 0, ng + h)),
                          pl.BlockSpec((None, T, wl), lambda b, h, i: (b, 0, 2 * ng + h))],
        out_specs=pl.BlockSpec((None, TM, wl), lambda b, h, i: (b, i, h)),
        out_shape=jax.ShapeDtypeStruct((B, rows, D), BF16),
        scratch_shapes=scratch(T),
        compiler_params=_cparams("parallel", "parallel", "parallel"),
        name="da_attn",
    )(lamv, subln_g, qkv, qkv, qkv)
    if not need_ctx:
        return out
    return pl.pallas_call(
        functools.partial(_da_attn_kernel, lam_init=lam_init, nk=CTX),
        grid=(B, ng),
        in_specs=small + [pl.BlockSpec((None, TM, wl), lambda b, h: (b, 0, h)),
                          pl.BlockSpec((None, CTX, wl), lambda b, h: (b, 0, ng + h)),
                          pl.BlockSpec((None, CTX, wl), lambda b, h: (b, 0, 2 * ng + h)),
                          pl.BlockSpec(memory_space=pl.ANY)],
        out_specs=pl.BlockSpec((None, TM, wl), lambda b, h: (b, 0, h)),
        out_shape=jax.ShapeDtypeStruct((B, rows, D), BF16),
        input_output_aliases={5: 0},
        scratch_shapes=scratch(CTX),
        compiler_params=_cparams("parallel", "parallel"),
        name="da_attn_ctx",
    )(lamv, subln_g, qkv, qkv, qkv, out)


def _swa_kernel(sink_ref, q_ref, k_ref, v_ref, *rest, is_lat):
    o_ref, s_scr, acc_scr = rest[-3:]
    nwin = 3 * SW_QB
    nq = SW_GRP * SW_QB
    grp = lax.broadcasted_iota(jnp.int32, (1, nq), 1) // SW_QB

    def body():
        qs = jnp.concatenate([q_ref[:, g * 256:(g + 1) * 256] for g in range(SW_GRP)], axis=0)
        kc = k_ref[0:CTX, :]
        vc = v_ref[0:CTX, :]
        lane_c = lax.broadcasted_iota(jnp.int32, (CTX, 256), 1) // HEAD_DIM
        if is_lat:
            j = pl.program_id(1) - CTX // SW_QB
            start = pl.multiple_of(jnp.minimum(CTX + (j - 1) * SW_QB, T - nwin), SW_QB)
            kw = k_ref[pl.ds(start, nwin), :]
            vw = v_ref[pl.ds(start, nwin), :]
            lane_w = lax.broadcasted_iota(jnp.int32, (nwin, 256), 1) // HEAD_DIM
            kpos = start - CTX + lax.broadcasted_iota(jnp.int32, (nwin, nq), 0)
            qpos = j * SW_QB + lax.broadcasted_iota(jnp.int32, (nwin, nq), 1) % SW_QB
            band = (jnp.abs(qpos - kpos) <= SW_WIN) & (kpos >= 0)
        nkeys = CTX + nwin if is_lat else CTX
        for h in range(SW_KV):
            kcz = jnp.where(lane_c == h, kc, jnp.zeros_like(kc))
            s_scr[h, 0:CTX, :] = lax.dot_general(kcz, qs, NT_DIMS, preferred_element_type=F32)
            if is_lat:
                kwz = jnp.where(lane_w == h, kw, jnp.zeros_like(kw))
                s_w = lax.dot_general(kwz, qs, NT_DIMS, preferred_element_type=F32)
                s_scr[h, CTX:nkeys, :] = jnp.where(band, s_w, NEG)
        vall = jnp.concatenate([vc, vw], axis=0) if is_lat else vc
        for h in range(SW_KV):
            sink = jnp.zeros((1, nq), F32)
            for g in range(SW_GRP):
                sg = sink_ref[0:1, SW_GRP * h + g:SW_GRP * h + g + 1] * LOG2E
                sink = jnp.where(grp == g, sg, sink)
            s = s_scr[h, 0:nkeys, :]
            m = jnp.maximum(jnp.max(s, axis=0, keepdims=True), sink)
            e = jnp.exp2(s - m)
            r = 1.0 / (jnp.sum(e, axis=0, keepdims=True) + jnp.exp2(sink - m))
            rows = slice(h * HEAD_DIM, (h + 1) * HEAD_DIM)
            acc_scr[rows, :] = lax.dot_general(vall[:, rows], (e * r).astype(BF16), TN_DIMS,
                                               preferred_element_type=F32)
        for g in range(SW_GRP):
            o_ref[:, g * 256:(g + 1) * 256] = acc_scr[:, g * SW_QB:(g + 1) * SW_QB].T.astype(BF16)

    if is_lat:
        @pl.when(pl.program_id(1) < CTX // SW_QB)
        def _():
            o_ref[...] = jnp.zeros(o_ref.shape, o_ref.dtype)

        pl.when(pl.program_id(1) >= CTX // SW_QB)(body)
    else:
        body()


def _swa_attn(qkv, sink):
    cblk = CTX // SW_QB
    scratch = lambda nkeys: [pltpu.VMEM((SW_KV, nkeys, SW_GRP * SW_QB), F32),
                             pltpu.VMEM((SW_KV * HEAD_DIM, SW_GRP * SW_QB), F32)]
    out = pl.pallas_call(
        functools.partial(_swa_kernel, is_lat=True),
        grid=(B, T // SW_QB),
        in_specs=[_resident((1, 16)),
                  pl.BlockSpec((None, SW_QB, D), lambda b, i: (b, i, 0)),
                  pl.BlockSpec((None, T, 256), lambda b, i: (b, 0, 4)),
                  pl.BlockSpec((None, T, 256), lambda b, i: (b, 0, 5))],
        out_specs=pl.BlockSpec((None, SW_QB, D), lambda b, i: (b, i, 0)),
        out_shape=jax.ShapeDtypeStruct((B, T, D), BF16),
        scratch_shapes=scratch(CTX + 3 * SW_QB),
        compiler_params=_cparams("parallel", "parallel"),
        name="swa_attn",
    )(sink, qkv, qkv, qkv)
    return pl.pallas_call(
        functools.partial(_swa_kernel, is_lat=False),
        grid=(B, cblk),
        in_specs=[_resident((1, 16)),
                  pl.BlockSpec((None, SW_QB, D), lambda b, i: (b, i, 0)),
                  pl.BlockSpec((None, CTX, 256), lambda b, i: (b, 0, 4)),
                  pl.BlockSpec((None, CTX, 256), lambda b, i: (b, 0, 5)),
                  pl.BlockSpec(memory_space=pl.ANY)],
        out_specs=pl.BlockSpec((None, SW_QB, D), lambda b, i: (b, i, 0)),
        out_shape=jax.ShapeDtypeStruct((B, T, D), BF16),
        input_output_aliases={4: 0},
        scratch_shapes=scratch(CTX),
        compiler_params=_cparams("parallel", "parallel"),
        name="swa_attn_ctx",
    )(sink, qkv, qkv, qkv, out)


def _logsig(x):
    return jnp.minimum(x, 0.0) - jnp.log(1.0 + jnp.exp(-jnp.abs(x)))


def _lane_cumsum(x, reverse):
    n = x.shape[1]
    lane = lax.broadcasted_iota(jnp.int32, x.shape, 1)
    k = 1
    while k < n:
        if reverse:
            x = x + jnp.where(lane < n - k, pltpu.roll(x, n - k, 1), 0.0)
        else:
            x = x + jnp.where(lane >= k, pltpu.roll(x, k, 1), 0.0)
        k *= 2
    return x


def _mlstm_kernel(qkf_ref, vtf_ref, grf_ref, qkb_ref, vtb_ref, grb_ref, hf_ref, hb_ref, st_ref, m_ref):
    step = pl.program_id(1)

    @pl.when(step == 0)
    def _():
        st_ref[...] = jnp.zeros_like(st_ref)
        m_ref[...] = jnp.zeros_like(m_ref)

    L = TM
    row = lax.broadcasted_iota(jnp.int32, (L, L), 0)
    col = lax.broadcasted_iota(jnp.int32, (L, L), 1)
    lane = lax.broadcasted_iota(jnp.int32, (L, 128), 1)
    ones_t = jnp.where(lax.broadcasted_iota(jnp.int32, (ML_V, L), 0) == 0, 1.0, 0.0).astype(BF16)

    dirs = ((qkf_ref, vtf_ref, grf_ref, hf_ref), (qkb_ref, vtb_ref, grb_ref, hb_ref))
    for d, (qk_ref, vt_ref, gr_ref, h_ref) in enumerate(dirs):
        vis = col >= row if d == 0 else col <= row
        last = L - 1 if d == 0 else 0
        gr = gr_ref[...]
        cum_r = _lane_cumsum(_logsig(gr[16 * d + 8:16 * d + 16, :]), reverse=d == 1)
        ci_r = gr[16 * d:16 * d + 8, :] - cum_r
        ci_all = jnp.concatenate([ci_r, jnp.zeros((128 - ML_HEADS, L), F32)], axis=0).T
        def decay_and_scores(h):
            idx = d * ML_HEADS + h
            p = h // 2
            q2 = qk_ref[:, p * 128:(p + 1) * 128]
            k2 = qk_ref[:, 512 + p * 128:512 + (p + 1) * 128]
            own = lane >= HEAD_DIM if h % 2 else lane < HEAD_DIM
            qh = jnp.where(own, q2, jnp.zeros_like(q2))
            b_r = cum_r[h:h + 1, :]
            m_prev = m_ref[idx][0:1, 0:1]
            st = st_ref[idx]
            dm = jnp.where(vis, ci_all[:, h:h + 1] + b_r, NEG)
            a_r = b_r + m_prev
            m_r = jnp.maximum(a_r, jnp.max(dm, axis=0, keepdims=True))
            inter = jnp.exp(a_r - m_r)
            sc = lax.dot_general(k2, qh, NT_DIMS, preferred_element_type=F32) * jnp.exp(dm - m_r)
            carry = lax.dot_general(st.astype(BF16), qh, NT_DIMS, preferred_element_type=F32)
            return k2, b_r, m_prev, st, m_r, inter, sc, carry

        pending = decay_and_scores(0)
        for h in range(ML_HEADS):
            idx = d * ML_HEADS + h
            k2, b_r, m_prev, st, m_r, inter, sc, carry = pending
            if h + 1 < ML_HEADS:
                pending = decay_and_scores(h + 1)
            vt = vt_ref[h * ML_V:(h + 1) * ML_V, :]
            i_r = gr[16 * d + h:16 * d + h + 1, :]
            num = inter * carry[0:ML_V, :] + jnp.dot(vt, sc.astype(BF16), preferred_element_type=F32)
            den = inter * carry[ML_V:ML_V + 1, :] + jnp.sum(sc, axis=0, keepdims=True)
            ht = num / jnp.maximum(jnp.abs(den), jnp.exp(-m_r))
            h_ref[:, h * ML_V:(h + 1) * ML_V] = ht.T

            m_new = m_r[:, last:last + 1]
            b_last = b_r[:, last:last + 1]
            g_r = jnp.exp(b_last - b_r + i_r - m_new)
            decay = jnp.exp(b_last + m_prev - m_new)
            gvt = (jnp.concatenate([vt, ones_t], axis=0).astype(F32) * g_r).astype(BF16)
            st_ref[idx] = decay * st + jnp.dot(gvt, k2, preferred_element_type=F32)
            m_ref[idx] = jnp.broadcast_to(m_new, (8, 128))


def _mlstm(qk, vt, gr):
    fwd = lambda b, s: (b, s, 0)
    bwd = lambda b, s: (b, jnp.where(s == 0, 0, NT - s), 0)
    fwd_t = lambda b, s: (b, 0, s)
    bwd_t = lambda b, s: (b, 0, jnp.where(s == 0, 0, NT - s))
    tok = lambda n, ix: pl.BlockSpec((None, TM, n), ix)
    tok_t = lambda n, ix: pl.BlockSpec((None, n, TM), ix)
    return pl.pallas_call(
        _mlstm_kernel,
        grid=(B, NT),
        in_specs=[tok(D, fwd), tok_t(D, fwd_t), tok_t(32, fwd_t),
                  tok(D, bwd), tok_t(D, bwd_t), tok_t(32, bwd_t)],
        out_specs=[tok(D, fwd), tok(D, bwd)],
        out_shape=[jax.ShapeDtypeStruct((B, T, D), F32)] * 2,
        scratch_shapes=[pltpu.VMEM((2 * ML_HEADS, 2 * ML_V, 128), F32),
                        pltpu.VMEM((2 * ML_HEADS, 8, 128), F32)],
        compiler_params=_cparams("parallel", "arbitrary"),
        name="mlstm",
    )(qk, vt, gr, qk, vt, gr)


def _ml_gate_kernel(hf_ref, hb_ref, og_ref, ng_ref, a_ref):
    for h in range(ML_HEADS):
        sl = slice(h * ML_V, (h + 1) * ML_V)
        y = _rms(hf_ref[:, sl] + hb_ref[:, sl], ng_ref[:, sl])
        a_ref[:, sl] = (y * jax.nn.sigmoid(og_ref[:, sl])).astype(BF16)


def _ml_gate(hf, hb, og, ng):
    tile = pl.BlockSpec((None, TM, D), lambda b, i: (b, i, 0))
    return pl.pallas_call(
        _ml_gate_kernel,
        grid=(B, NT),
        in_specs=[tile, tile, tile, _resident((1, D))],
        out_specs=tile,
        out_shape=jax.ShapeDtypeStruct((B, T, D), BF16),
        compiler_params=_cparams("parallel", "parallel"),
        name="ml_gate",
    )(hf, hb, og, ng)


def _ffn_kernel(*refs, has_ctx, nt, final):
    per_tile = 7
    shared = refs[FFN_TPS * per_tile:FFN_TPS * per_tile + 7]
    o_ref = refs[FFN_TPS * per_tile + 7]
    h_scr, x_scr, ua_scr, ug_scr, act_scr = refs[FFN_TPS * per_tile + 8:]
    tiles = []
    for k in range(FFN_TPS):
        tile = pl.program_id(0) * FFN_TPS + k
        tiles.append(_FfnTile(tile % nt, *refs[k * per_tile:(k + 1) * per_tile], *shared,
                              o_ref.at[k * TM:(k + 1) * TM, :], h_scr.at[k], x_scr.at[k], ua_scr.at[k],
                              ug_scr.at[k], act_scr.at[k], has_ctx=has_ctx, nt=nt, final=final))
    nchunk = FFN // FFN_CHUNK
    tiles[0].prologue()
    for k, t in enumerate(tiles):
        for c in range(nchunk):
            if c == nchunk // 2 and k + 1 < FFN_TPS:
                tiles[k + 1].prologue()
            t.chunk(c)
        t.finish()


class _FfnTile:
    def __init__(self, i, a_ref, ap_ref, an_ref, x_ref, xp_ref, xn_ref, mod_ref, wo_ref, g_ref, wup_ref, cw_ref,
                 cb_ref, wdn_ref, fg_ref, o_ref, h_scr, x_scr, ua_scr, ug_scr, act_scr, *, has_ctx, nt, final):
        self.__dict__.update(locals())

    def prologue(self):
        s = self
        mod_ref = s.mod_ref
        g = s.g_ref[...]
        shift, scale, gate1 = mod_ref[3:4, :], mod_ref[4:5, :], mod_ref[2:3, :]
        sub = lax.broadcasted_iota(jnp.int32, (FFN_EDGE, D), 0)
        a_prev = s.ap_ref[...].astype(F32)[FFN_EDGE - 1:FFN_EDGE, :]
        a_next = s.an_ref[...].astype(F32)[0:1, :]
        a_edge = jnp.where(sub == 0, a_prev, jnp.where(sub == 1, a_next, 0.0)).astype(BF16)
        y1 = jnp.dot(jnp.concatenate([s.a_ref[...], a_edge], axis=0), s.wo_ref[...], preferred_element_type=F32)
        x1 = s.x_ref[...] + gate1 * y1[0:TM, :]
        x1_prev = s.xp_ref[HALO - 1:HALO, :] + gate1 * y1[TM:TM + 1, :]
        x1_next = s.xn_ref[0:1, :] + gate1 * y1[TM + 1:TM + 2, :]
        if s.has_ctx:
            prev_ok = s.i >= 2
            next_ok = (s.i != 0) & (s.i != s.nt - 1)
        else:
            prev_ok = s.i >= 1
            next_ok = s.i != s.nt - 1
        x = jnp.swapaxes(x1.reshape(8, TM // 8, D), 0, 1).reshape(TM, D)
        s.x_scr[...] = x
        s.h_scr[0:TM, :] = _norm_mod(x, g, shift, scale)
        h_prev = jnp.where(prev_ok, _norm_mod(x1_prev, g, shift, scale), 0.0)
        h_next = jnp.where(next_ok, _norm_mod(x1_next, g, shift, scale), 0.0)
        s.h_scr[TM:, :] = jnp.where(sub == 0, h_prev, jnp.where(sub == 1, h_next, 0.0))
        s.hb = s.h_scr[...].astype(BF16)

    def _conv(self, scr, off):
        sub8 = lax.broadcasted_iota(jnp.int32, (8, FFN_CHUNK), 0)
        w = self.cw_ref[:, off:off + FFN_CHUNK]
        first = jnp.where(sub8 == 0, scr[TM:TM + 1, :], pltpu.roll(scr[TM - 8:TM, :], 1, 0))
        final_grp = jnp.where(sub8 == 7, scr[TM + 1:TM + 2, :], pltpu.roll(scr[0:8, :], 7, 0))
        prev = jnp.concatenate([first, scr[0:TM - 8, :]], axis=0)
        nxt = jnp.concatenate([scr[8:TM, :], final_grp], axis=0)
        return prev * w[0:1] + scr[0:TM, :] * w[1:2] + nxt * w[2:3] + self.cb_ref[:, off:off + FFN_CHUNK]

    def chunk(self, c):
        s = self
        lo = c * FFN_CHUNK
        s.ua_scr[c % 2] = jnp.dot(s.hb, s.wup_ref[:, lo:lo + FFN_CHUNK], preferred_element_type=F32)
        s.ug_scr[c % 2] = jnp.dot(s.hb, s.wup_ref[:, FFN + lo:FFN + lo + FFN_CHUNK], preferred_element_type=F32)
        a = s._conv(s.ua_scr.at[c % 2], lo)
        gg = s._conv(s.ug_scr.at[c % 2], FFN + lo)
        s.act_scr[:, lo:lo + FFN_CHUNK] = (a * (gg * jax.nn.sigmoid(gg))).astype(BF16)

    def finish(self):
        s = self
        y = jnp.dot(s.act_scr[...], s.wdn_ref[...], preferred_element_type=F32)
        out = s.x_scr[...] + s.mod_ref[5:6, :] * y
        if s.final:
            out = _rms(out, s.fg_ref[...])
        s.o_ref[...] = jnp.swapaxes(out.reshape(TM // 8, 8, D), 0, 1).reshape(TM, D)


def _ffn(a, wo, xs, mod, g, wup, cw, cb, wdn, fg, has_ctx, final):
    rows = a.shape[1]
    nt = rows // TM
    off = 0 if has_ctx else 1
    ntiles = B * nt
    a2 = a.reshape(B * rows, D)
    x2 = xs.reshape(B * T, D)
    x_per_tile = TM // HALO
    a_per_tile = TM // FFN_EDGE
    x_last = B * T // HALO - 1
    a_last = B * rows // FFN_EDGE - 1

    def tile_specs(k):
        ta = lambda j: j * FFN_TPS + k
        tx = lambda j: (ta(j) // nt) * NT + ta(j) % nt + off
        if has_ctx:
            mod_ix = lambda j: (jnp.where(ta(j) % nt == 0, B, ta(j) // nt), 0, 0)
        else:
            mod_ix = lambda j: (ta(j) // nt, 0, 0)
        return [pl.BlockSpec((TM, D), lambda j: (ta(j), 0)),
                pl.BlockSpec((FFN_EDGE, D), lambda j: (jnp.maximum(ta(j) * a_per_tile - 1, 0), 0)),
                pl.BlockSpec((FFN_EDGE, D), lambda j: (jnp.minimum((ta(j) + 1) * a_per_tile, a_last), 0)),
                pl.BlockSpec((TM, D), lambda j: (tx(j), 0)),
                pl.BlockSpec((HALO, D), lambda j: (jnp.maximum(tx(j) * x_per_tile - 1, 0), 0)),
                pl.BlockSpec((HALO, D), lambda j: (jnp.minimum((tx(j) + 1) * x_per_tile, x_last), 0)),
                pl.BlockSpec((None, 6, D), mod_ix)]

    per_tile_specs = [spec for k in range(FFN_TPS) for spec in tile_specs(k)]
    per_tile_args = [arr for _ in range(FFN_TPS) for arr in (a2, a2, a2, x2, x2, x2, mod)]
    out = pl.pallas_call(
        functools.partial(_ffn_kernel, has_ctx=has_ctx, nt=nt, final=final),
        grid=(ntiles // FFN_TPS,),
        in_specs=per_tile_specs + [_resident((D, D)),
                                   _resident((1, D)),
                                   _resident((D, 2 * FFN)),
                                   _resident((3, 2 * FFN)),
                                   _resident((1, 2 * FFN)),
                                   _resident((FFN, D)),
                                   _resident((1, D))],
        out_specs=pl.BlockSpec((FFN_TPS * TM, D), lambda j: (j, 0)),
        out_shape=jax.ShapeDtypeStruct((B * rows, D), F32),
        scratch_shapes=[pltpu.VMEM((FFN_TPS, TM + FFN_EDGE, D), F32),
                        pltpu.VMEM((FFN_TPS, TM, D), F32),
                        pltpu.VMEM((FFN_TPS, 2, TM + FFN_EDGE, FFN_CHUNK), F32),
                        pltpu.VMEM((FFN_TPS, 2, TM + FFN_EDGE, FFN_CHUNK), F32),
                        pltpu.VMEM((FFN_TPS, TM, FFN), BF16)],
        compiler_params=_cparams("parallel"),
        name="ffn",
    )(*per_tile_args, wo, g, wup, cw, cb, wdn, fg)
    return out.reshape(B, rows, D)


def _rope_tables():
    rows = S // GRID_W
    quarter = HEAD_DIM // 4
    row = jnp.repeat(jnp.arange(rows, dtype=F32), GRID_W)
    col = jnp.tile(jnp.arange(GRID_W, dtype=F32), rows)
    inv = 10000.0 ** (-jnp.arange(quarter, dtype=F32) / quarter)
    ang_r = row[:, None] * inv
    ang_c = col[:, None] * inv
    cr, sr, cc, sc = jnp.cos(ang_r), jnp.sin(ang_r), jnp.cos(ang_c), jnp.sin(ang_c)
    z = jnp.zeros_like(sr)
    cos = jnp.concatenate([cr, cr, cc, cc], axis=1)
    sa = jnp.concatenate([-sr, z, -sc, z], axis=1)
    sb = jnp.concatenate([z, sr, z, sc], axis=1)
    pad = lambda t, v: jnp.concatenate([jnp.full((CTX, 128), v, F32), jnp.tile(t, (1, 2))], axis=0)
    return pad(cos, 1.0), pad(sa, 0.0), pad(sb, 0.0)


def _group_major(w, axis):
    shape = w.shape
    w = w.reshape(shape[:axis] + (SW_KV, SW_GRP, HEAD_DIM) + shape[axis + 1:])
    w = jnp.swapaxes(w, axis, axis + 1)
    return w.reshape(shape)


def _row(v):
    return v.reshape(1, -1)


def _bf(w):
    return w.astype(BF16)


def _mod(cpad, w, b):
    return _ada(cpad, w, b).reshape(MOD_ROWS, 6, D)


def _da_layer(idx, xs, mod, rope, norm1_g, w_in, lq1, lk1, lq2, lk2, subln_g, w_out):
    need_ctx = idx < DEPTH - 1
    lam_init = 0.8 - 0.6 * math.exp(-0.3 * idx)
    qkv = _attn_proj(xs, mod, _row(norm1_g), _bf(w_in), rope, 4, 4, 4, v_channel_major=False)
    o = _da_attn(qkv, jnp.stack([lq1, lk1, lq2, lk2]), _row(subln_g), lam_init, need_ctx)
    return o, _bf(w_out)


def _ml_layer(xs, mod, norm1_g, w_in, gate_b, norm_g, w_out):
    qk, vt, og, gr = _ml_proj(xs, mod, _row(norm1_g), _bf(w_in[:, :3072]), _bf(w_in[:, 3072:]).T,
                              gate_b.reshape(-1, 1))
    hf, hb = _mlstm(qk, vt, gr)
    return _ml_gate(hf, hb, og, _row(norm_g)), _bf(w_out)


def _sw_layer(xs, mod, rope, norm1_g, w_in, sink, w_out):
    w_q = _group_major(w_in[:, :D], 1)
    qkv = _attn_proj(xs, mod, _row(norm1_g), _bf(jnp.concatenate([w_q, w_in[:, D:]], axis=1)), rope, 4, 1, 1,
                     v_channel_major=False)
    o = _swa_attn(qkv, _row(sink))
    return o, _bf(_group_major(w_out, 0))


def _ffn_layer(idx, mix, xs, mod, norm2_g, w_up, conv_w, conv_b, w_down, final_norm_g):
    final = idx == DEPTH - 1
    return _ffn(mix[0], mix[1], xs, mod, _row(norm2_g), _bf(w_up), conv_w, _row(conv_b), _bf(w_down),
                _row(final_norm_g), has_ctx=not final, final=final)


def kernel(x, c, ctx, c_ctx, l0_ada_w, l0_ada_b, l0_norm1_g, l0_da_w_in, l0_da_lam_q1, l0_da_lam_k1, l0_da_lam_q2, l0_da_lam_k2, l0_da_subln_g, l0_da_w_out, l0_norm2_g, l0_ffn_w_up, l0_ffn_conv_w, l0_ffn_conv_b, l0_ffn_w_down, l1_ada_w, l1_ada_b, l1_norm1_g, l1_ml_w_in, l1_ml_gate_b, l1_ml_norm_g, l1_ml_w_out, l1_norm2_g, l1_ffn_w_up, l1_ffn_conv_w, l1_ffn_conv_b, l1_ffn_w_down, l2_ada_w, l2_ada_b, l2_norm1_g, l2_sw_w_in, l2_sw_sink, l2_sw_w_out, l2_norm2_g, l2_ffn_w_up, l2_ffn_conv_w, l2_ffn_conv_b, l2_ffn_w_down, l3_ada_w, l3_ada_b, l3_norm1_g, l3_da_w_in, l3_da_lam_q1, l3_da_lam_k1, l3_da_lam_q2, l3_da_lam_k2, l3_da_subln_g, l3_da_w_out, l3_norm2_g, l3_ffn_w_up, l3_ffn_conv_w, l3_ffn_conv_b, l3_ffn_w_down, final_norm_g):
    rope = _rope_tables()
    xs = jnp.concatenate([ctx, x], axis=1)
    cpad = jnp.concatenate([c, c_ctx[None, :], jnp.zeros((MOD_ROWS - B - 1, D), F32)], axis=0)

    m = _mod(cpad, l0_ada_w, l0_ada_b)
    mix = _da_layer(0, xs, m, rope, l0_norm1_g, l0_da_w_in, l0_da_lam_q1, l0_da_lam_k1, l0_da_lam_q2, l0_da_lam_k2,
                    l0_da_subln_g, l0_da_w_out)
    xs = _ffn_layer(0, mix, xs, m, l0_norm2_g, l0_ffn_w_up, l0_ffn_conv_w, l0_ffn_conv_b, l0_ffn_w_down, final_norm_g)

    m = _mod(cpad, l1_ada_w, l1_ada_b)
    mix = _ml_layer(xs, m, l1_norm1_g, l1_ml_w_in, l1_ml_gate_b, l1_ml_norm_g, l1_ml_w_out)
    xs = _ffn_layer(1, mix, xs, m, l1_norm2_g, l1_ffn_w_up, l1_ffn_conv_w, l1_ffn_conv_b, l1_ffn_w_down, final_norm_g)

    m = _mod(cpad, l2_ada_w, l2_ada_b)
    mix = _sw_layer(xs, m, rope, l2_norm1_g, l2_sw_w_in, l2_sw_sink, l2_sw_w_out)
    xs = _ffn_layer(2, mix, xs, m, l2_norm2_g, l2_ffn_w_up, l2_ffn_conv_w, l2_ffn_conv_b, l2_ffn_w_down, final_norm_g)

    m = _mod(cpad, l3_ada_w, l3_ada_b)
    mix = _da_layer(3, xs, m, rope, l3_norm1_g, l3_da_w_in, l3_da_lam_q1, l3_da_lam_k1, l3_da_lam_q2, l3_da_lam_k2,
                    l3_da_subln_g, l3_da_w_out)
    return _ffn_layer(3, mix, xs, m, l3_norm2_g, l3_ffn_w_up, l3_ffn_conv_w, l3_ffn_conv_b, l3_ffn_w_down,
                      final_norm_g)
```

```python
import functools
import math

import jax
import jax.numpy as jnp
from jax import lax
from jax.experimental import pallas as pl
from jax.experimental.pallas import tpu as pltpu

F32 = jnp.float32
BF16 = jnp.bfloat16

D = 1024
B = 8
S = 2048
CTX = 256
T = CTX + S
DEPTH = 4
GRID_W = 64
HEAD_DIM = 64
EPS = 1e-6
NEG = -1e30

TM = 256
NT = T // TM
NS = S // TM
MOD_ROWS = 16
DA_HEADS = 8
DA_HPS = 4
DA_SLABS = 6
DA_CHUNK = 256
ML_HEADS = 8
ML_V = 128
SW_KV = 4
SW_GRP = 4
SW_WIN = 128
SW_QB = 128
FFN = 2816
FFN_CHUNK = 256
HALO = 8
FFN_EDGE = 16
FFN_TPS = 2
PROJ_TPS = 2

LOG2E = math.log2(math.e)
Q_SCALE = HEAD_DIM ** -0.5 * LOG2E

NT_DIMS = (((1,), (1,)), ((), ()))
TN_DIMS = (((0,), (0,)), ((), ()))

VMEM_LIMIT = 56 * 1024 * 1024


def _cparams(*sem):
    return pltpu.CompilerParams(dimension_semantics=sem, vmem_limit_bytes=VMEM_LIMIT)


def _resident(shape):
    zeros = (0,) * len(shape)
    return pl.BlockSpec(shape, lambda *_: zeros, pipeline_mode=pl.Buffered(1))


def _rms(x, g):
    ms = jnp.mean(x * x, axis=-1, keepdims=True)
    return x * lax.rsqrt(ms + EPS) * g


def _norm_mod(x, g, shift, scale):
    return _rms(x, g) * (1.0 + scale) + shift


def _mod_ix_ctx(b, i):
    return (jnp.where(i == 0, B, b), 0, 0)


def _mod_ix_lat(b, i):
    return (b, 0, 0)


def _ada_kernel(c_ref, w_ref, b_ref, o_ref):
    c = c_ref[...]
    a = (c * jax.nn.sigmoid(c)).astype(BF16)
    o_ref[...] = jnp.dot(a, w_ref[...].astype(BF16), preferred_element_type=F32) + b_ref[...]


def _ada(cpad, w, bias):
    n = w.shape[1]
    tn = 1024
    return pl.pallas_call(
        _ada_kernel,
        grid=(n // tn,),
        in_specs=[pl.BlockSpec((MOD_ROWS, D), lambda j: (0, 0)),
                  pl.BlockSpec((D, tn), lambda j: (0, j)),
                  pl.BlockSpec((1, tn), lambda j: (0, j))],
        out_specs=pl.BlockSpec((MOD_ROWS, tn), lambda j: (0, j)),
        out_shape=jax.ShapeDtypeStruct((MOD_ROWS, n), F32),
        compiler_params=_cparams("arbitrary"),
        name="ada",
    )(cpad, w, bias.reshape(1, n))


def _rope(c, cos, sa, sb):
    return c * cos + pltpu.roll(c, 112, 1) * sa + pltpu.roll(c, 16, 1) * sb


def _attn_proj_kernel(*refs, nq, nk, nv):
    per_tile = 5
    g_ref, w_ref, o_ref = refs[PROJ_TPS * per_tile:]
    nchunk = nq + nk + nv
    hs = [None] * PROJ_TPS

    def prologue(k):
        x_ref, mod_ref = refs[k * per_tile:k * per_tile + 2]
        hs[k] = _norm_mod(x_ref[...], g_ref[...], mod_ref[0:1, :], mod_ref[1:2, :]).astype(BF16)

    def chunk(k, j):
        cos_ref, sa_ref, sb_ref = refs[k * per_tile + 2:(k + 1) * per_tile]
        y = jnp.dot(hs[k], w_ref[:, j * 256:(j + 1) * 256], preferred_element_type=F32)
        if j < nq + nk:
            halves = []
            for t in range(2):
                r = _rope(y[:, t * 128:(t + 1) * 128], cos_ref[...], sa_ref[...], sb_ref[...])
                halves.append(r * Q_SCALE if j < nq else r)
            y = jnp.concatenate(halves, axis=1)
        o_ref[k * TM:(k + 1) * TM, j * 256:(j + 1) * 256] = y.astype(BF16)

    prologue(0)
    for k in range(PROJ_TPS):
        for j in range(nchunk):
            if j == nchunk // 2 and k + 1 < PROJ_TPS:
                prologue(k + 1)
            chunk(k, j)


def _attn_proj(xs, mod, g, w, rope, nq, nk, nv):
    n = (nq + nk + nv) * 256
    x2 = xs.reshape(B * T, D)

    def tile_specs(k):
        t = lambda j: j * PROJ_TPS + k
        tab = pl.BlockSpec((TM, 128), lambda j: (t(j) % NT, 0))
        return [pl.BlockSpec((TM, D), lambda j: (t(j), 0)),
                pl.BlockSpec((None, 6, D), lambda j: (jnp.where(t(j) % NT == 0, B, t(j) // NT), 0, 0)),
                tab, tab, tab]

    per_tile_specs = [spec for k in range(PROJ_TPS) for spec in tile_specs(k)]
    per_tile_args = [arr for _ in range(PROJ_TPS) for arr in (x2, mod) + tuple(rope)]
    out = pl.pallas_call(
        functools.partial(_attn_proj_kernel, nq=nq, nk=nk, nv=nv),
        grid=(B * NT // PROJ_TPS,),
        in_specs=per_tile_specs + [_resident((1, D)), _resident((D, n))],
        out_specs=pl.BlockSpec((PROJ_TPS * TM, n), lambda j: (j, 0)),
        out_shape=jax.ShapeDtypeStruct((B * T, n), BF16),
        compiler_params=_cparams("parallel"),
        name="attn_proj",
    )(*per_tile_args, g, w)
    return out.reshape(B, T, n)


def _ml_proj_kernel(x_ref, mod_ref, g_ref, w_ref, wgt_ref, gbt_ref, qk_ref, vt_ref, o_ref, gr_ref):
    h = _norm_mod(x_ref[...], g_ref[...], mod_ref[0:1, :], mod_ref[1:2, :]).astype(BF16)
    for j in range(12):
        y = jnp.dot(h, w_ref[:, j * 256:(j + 1) * 256], preferred_element_type=F32)
        if j < 2:
            qk_ref[:, j * 256:(j + 1) * 256] = (y * 0.125).astype(BF16)
        elif j < 4:
            qk_ref[:, j * 256:(j + 1) * 256] = y.astype(BF16)
        elif j < 8:
            vt_ref[(j - 4) * 256:(j - 3) * 256, :] = y.T.astype(BF16)
        else:
            o_ref[:, (j - 8) * 256:(j - 7) * 256] = y
    gr_ref[...] = lax.dot_general(wgt_ref[...], h, NT_DIMS, preferred_element_type=F32) + gbt_ref[...]


def _ml_proj(xs, mod, g, w, wgt, gbt):
    tile = lambda n, dt: (pl.BlockSpec((None, TM, n), lambda b, i: (b, i, 0)),
                          jax.ShapeDtypeStruct((B, T, n), dt))
    tile_t = lambda n, dt: (pl.BlockSpec((None, n, TM), lambda b, i: (b, 0, i)),
                            jax.ShapeDtypeStruct((B, n, T), dt))
    outs = [tile(D, BF16), tile_t(D, BF16), tile(D, F32), tile_t(32, F32)]
    return pl.pallas_call(
        _ml_proj_kernel,
        grid=(B, NT),
        in_specs=[pl.BlockSpec((None, TM, D), lambda b, i: (b, i, 0)),
                  pl.BlockSpec((None, 6, D), _mod_ix_ctx),
                  _resident((1, D)),
                  _resident((D, 3072)),
                  _resident((32, D)),
                  _resident((32, 1))],
        out_specs=[o[0] for o in outs],
        out_shape=[o[1] for o in outs],
        compiler_params=_cparams("parallel", "parallel"),
        name="ml_proj",
    )(xs, mod, g, w, wgt, gbt)


def _da_attn_kernel(lam_ref, g_ref, q_ref, k_ref, v_ref, *rest, lam_init, nk, zero_first_tile=False):
    o_ref, s_scr, w_scr = rest[-3:]
    lv = lam_ref[...]
    lam = (jnp.exp(jnp.sum(lv[0:1] * lv[1:2], axis=-1, keepdims=True))
           - jnp.exp(jnp.sum(lv[2:3] * lv[3:4], axis=-1, keepdims=True)) + lam_init)
    lane = lax.broadcasted_iota(jnp.int32, (TM, 128), 1)

    def attend(nk):
        chunk = min(DA_CHUNK, nk)
        nchunk = nk // chunk
        heads = [dict() for _ in range(DA_HPS)]

        def slab(hh, c):
            return (hh % 3) * 2 + c

        def stage1(hh, j):
            st = heads[hh]
            sl = slice(hh * 128, (hh + 1) * 128)
            rows = slice(j * chunk, (j + 1) * chunk)
            if j == 0:
                q = q_ref[:, sl]
                st["q"] = [jnp.where(lane < HEAD_DIM, q, jnp.zeros_like(q)),
                           jnp.where(lane >= HEAD_DIM, q, jnp.zeros_like(q))]
                st["m"] = [None, None]
                st["l"] = [None, None]
            for c in range(2):
                s = lax.dot_general(k_ref[rows, sl], st["q"][c], NT_DIMS, preferred_element_type=F32)
                s_scr[slab(hh, c), rows, :] = s
                mx = jnp.max(s, axis=0, keepdims=True)
                st["m"][c] = mx if j == 0 else jnp.maximum(st["m"][c], mx)

        def stage2(hh, j):
            st = heads[hh]
            rows = slice(j * chunk, (j + 1) * chunk)
            for c in range(2):
                e = jnp.exp2(s_scr[slab(hh, c), rows, :] - st["m"][c])
                s_scr[slab(hh, c), rows, :] = e
                sm = jnp.sum(e, axis=0, keepdims=True)
                st["l"][c] = sm if j == 0 else st["l"][c] + sm

        def stage3(hh, j):
            st = heads[hh]
            rows = slice(j * chunk, (j + 1) * chunk)
            if j == 0:
                st["coef"] = lam * st["l"][0] / st["l"][1]
            w = s_scr[slab(hh, 0), rows, :] - s_scr[slab(hh, 1), rows, :] * st["coef"]
            w_scr[hh % 2, rows, :] = w.astype(BF16)
            if j == nchunk - 1:
                sl = slice(hh * 128, (hh + 1) * 128)
                ot = lax.dot_general(v_ref[0:nk, sl], w_scr[hh % 2, 0:nk, :], TN_DIMS, preferred_element_type=F32)
                o = (ot * (1.0 / st["l"][0])).T
                o_ref[:, sl] = (_rms(o, g_ref[...]) * (1.0 - lam_init)).astype(BF16)

        for t in range(-1, DA_HPS + 1):
            for j in range(nchunk):
                if 0 <= t + 1 < DA_HPS:
                    stage1(t + 1, j)
                if 0 <= t < DA_HPS:
                    stage2(t, j)
                if 0 <= t - 1 < DA_HPS:
                    stage3(t - 1, j)

    if zero_first_tile:
        @pl.when(pl.program_id(2) == 0)
        def _():
            o_ref[...] = jnp.zeros(o_ref.shape, o_ref.dtype)

        pl.when(pl.program_id(2) > 0)(lambda: attend(nk))
    else:
        attend(nk)


def _da_attn(qkv, lamv, subln_g, lam_init, need_ctx):
    ng = DA_HEADS // DA_HPS
    wl = DA_HPS * 128
    out_off = 1 if need_ctx else 0
    rows = (NS + out_off) * TM
    scratch = lambda nk: [pltpu.VMEM((DA_SLABS, nk, TM), F32), pltpu.VMEM((2, nk, TM), BF16)]
    small = [_resident((4, HEAD_DIM)), _resident((1, 128))]
    out = pl.pallas_call(
        functools.partial(_da_attn_kernel, lam_init=lam_init, nk=T, zero_first_tile=need_ctx),
        grid=(B, ng, NS + out_off),
        in_specs=small + [pl.BlockSpec((None, TM, wl), lambda b, h, i: (b, i + 1 - out_off, h)),
                          pl.BlockSpec((None, T, wl), lambda b, h, i: (b, 0, ng + h)),
                          pl.BlockSpec((None, T, wl), lambda b, h, i: (b, 0, 2 * ng + h))],
        out_specs=pl.BlockSpec((None, TM, wl), lambda b, h, i: (b, i, h)),
        out_shape=jax.ShapeDtypeStruct((B, rows, D), BF16),
        scratch_shapes=scratch(T),
        compiler_params=_cparams("parallel", "parallel", "parallel"),
        name="da_attn",
    )(lamv, subln_g, qkv, qkv, qkv)
    if not need_ctx:
        return out
    return pl.pallas_call(
        functools.partial(_da_attn_kernel, lam_init=lam_init, nk=CTX),
        grid=(B, ng),
        in_specs=small + [pl.BlockSpec((None, TM, wl), lambda b, h: (b, 0, h)),
                          pl.BlockSpec((None, CTX, wl), lambda b, h: (b, 0, ng + h)),
                          pl.BlockSpec((None, CTX, wl), lambda b, h: (b, 0, 2 * ng + h)),
                          pl.BlockSpec(memory_space=pl.ANY)],
        out_specs=pl.BlockSpec((None, TM, wl), lambda b, h: (b, 0, h)),
        out_shape=jax.ShapeDtypeStruct((B, rows, D), BF16),
        input_output_aliases={5: 0},
        scratch_shapes=scratch(CTX),
        compiler_params=_cparams("parallel", "parallel"),
        name="da_attn_ctx",
    )(lamv, subln_g, qkv, qkv, qkv, out)


def _swa_kernel(sink_ref, q_ref, k_ref, v_ref, *rest, is_lat):
    o_ref, s_scr, acc_scr = rest[-3:]
    nwin = 3 * SW_QB
    nq = SW_GRP * SW_QB
    grp = lax.broadcasted_iota(jnp.int32, (1, nq), 1) // SW_QB

    def body():
        qs = jnp.concatenate([q_ref[:, g * 256:(g + 1) * 256] for g in range(SW_GRP)], axis=0)
        kc = k_ref[0:CTX, :]
        vc = v_ref[0:CTX, :]
        lane_c = lax.broadcasted_iota(jnp.int32, (CTX, 256), 1) // HEAD_DIM
        if is_lat:
            j = pl.program_id(1) - CTX // SW_QB
            start = pl.multiple_of(jnp.minimum(CTX + (j - 1) * SW_QB, T - nwin), SW_QB)
            kw = k_ref[pl.ds(start, nwin), :]
            vw = v_ref[pl.ds(start, nwin), :]
            lane_w = lax.broadcasted_iota(jnp.int32, (nwin, 256), 1) // HEAD_DIM
            kpos = start - CTX + lax.broadcasted_iota(jnp.int32, (nwin, nq), 0)
            qpos = j * SW_QB + lax.broadcasted_iota(jnp.int32, (nwin, nq), 1) % SW_QB
            band = (jnp.abs(qpos - kpos) <= SW_WIN) & (kpos >= 0)
        nkeys = CTX + nwin if is_lat else CTX
        for h in range(SW_KV):
            kcz = jnp.where(lane_c == h, kc, jnp.zeros_like(kc))
            s_scr[h, 0:CTX, :] = lax.dot_general(kcz, qs, NT_DIMS, preferred_element_type=F32)
            if is_lat:
                kwz = jnp.where(lane_w == h, kw, jnp.zeros_like(kw))
                s_w = lax.dot_general(kwz, qs, NT_DIMS, preferred_element_type=F32)
                s_scr[h, CTX:nkeys, :] = jnp.where(band, s_w, NEG)
        vall = jnp.concatenate([vc, vw], axis=0) if is_lat else vc
        for h in range(SW_KV):
            sink = jnp.zeros((1, nq), F32)
            for g in range(SW_GRP):
                sg = sink_ref[0:1, SW_GRP * h + g:SW_GRP * h + g + 1] * LOG2E
                sink = jnp.where(grp == g, sg, sink)
            s = s_scr[h, 0:nkeys, :]
            m = jnp.maximum(jnp.max(s, axis=0, keepdims=True), sink)
            e = jnp.exp2(s - m)
            r = 1.0 / (jnp.sum(e, axis=0, keepdims=True) + jnp.exp2(sink - m))
            rows = slice(h * HEAD_DIM, (h + 1) * HEAD_DIM)
            acc_scr[rows, :] = lax.dot_general(vall[:, rows], (e * r).astype(BF16), TN_DIMS,
                                               preferred_element_type=F32)
        for g in range(SW_GRP):
            o_ref[:, g * 256:(g + 1) * 256] = acc_scr[:, g * SW_QB:(g + 1) * SW_QB].T.astype(BF16)

    if is_lat:
        @pl.when(pl.program_id(1) < CTX // SW_QB)
        def _():
            o_ref[...] = jnp.zeros(o_ref.shape, o_ref.dtype)

        pl.when(pl.program_id(1) >= CTX // SW_QB)(body)
    else:
        body()


def _swa_attn(qkv, sink):
    cblk = CTX // SW_QB
    scratch = lambda nkeys: [pltpu.VMEM((SW_KV, nkeys, SW_GRP * SW_QB), F32),
                             pltpu.VMEM((SW_KV * HEAD_DIM, SW_GRP * SW_QB), F32)]
    out = pl.pallas_call(
        functools.partial(_swa_kernel, is_lat=True),
        grid=(B, T // SW_QB),
        in_specs=[_resident((1, 16)),
                  pl.BlockSpec((None, SW_QB, D), lambda b, i: (b, i, 0)),
                  pl.BlockSpec((None, T, 256), lambda b, i: (b, 0, 4)),
                  pl.BlockSpec((None, T, 256), lambda b, i: (b, 0, 5))],
        out_specs=pl.BlockSpec((None, SW_QB, D), lambda b, i: (b, i, 0)),
        out_shape=jax.ShapeDtypeStruct((B, T, D), BF16),
        scratch_shapes=scratch(CTX + 3 * SW_QB),
        compiler_params=_cparams("parallel", "parallel"),
        name="swa_attn",
    )(sink, qkv, qkv, qkv)
    return pl.pallas_call(
        functools.partial(_swa_kernel, is_lat=False),
        grid=(B, cblk),
        in_specs=[_resident((1, 16)),
                  pl.BlockSpec((None, SW_QB, D), lambda b, i: (b, i, 0)),
                  pl.BlockSpec((None, CTX, 256), lambda b, i: (b, 0, 4)),
                  pl.BlockSpec((None, CTX, 256), lambda b, i: (b, 0, 5)),
                  pl.BlockSpec(memory_space=pl.ANY)],
        out_specs=pl.BlockSpec((None, SW_QB, D), lambda b, i: (b, i, 0)),
        out_shape=jax.ShapeDtypeStruct((B, T, D), BF16),
        input_output_aliases={4: 0},
        scratch_shapes=scratch(CTX),
        compiler_params=_cparams("parallel", "parallel"),
        name="swa_attn_ctx",
    )(sink, qkv, qkv, qkv, out)


def _logsig(x):
    return jnp.minimum(x, 0.0) - jnp.log(1.0 + jnp.exp(-jnp.abs(x)))


def _lane_cumsum(x, reverse):
    n = x.shape[1]
    lane = lax.broadcasted_iota(jnp.int32, x.shape, 1)
    k = 1
    while k < n:
        if reverse:
            x = x + jnp.where(lane < n - k, pltpu.roll(x, n - k, 1), 0.0)
        else:
            x = x + jnp.where(lane >= k, pltpu.roll(x, k, 1), 0.0)
        k *= 2
    return x


def _mlstm_kernel(qkf_ref, vtf_ref, grf_ref, qkb_ref, vtb_ref, grb_ref, hf_ref, hb_ref, st_ref, m_ref):
    step = pl.program_id(1)

    @pl.when(step == 0)
    def _():
        st_ref[...] = jnp.zeros_like(st_ref)
        m_ref[...] = jnp.zeros_like(m_ref)

    L = TM
    row = lax.broadcasted_iota(jnp.int32, (L, L), 0)
    col = lax.broadcasted_iota(jnp.int32, (L, L), 1)
    lane = lax.broadcasted_iota(jnp.int32, (L, 128), 1)
    ones_t = jnp.where(lax.broadcasted_iota(jnp.int32, (ML_V, L), 0) == 0, 1.0, 0.0).astype(BF16)

    dirs = ((qkf_ref, vtf_ref, grf_ref, hf_ref), (qkb_ref, vtb_ref, grb_ref, hb_ref))
    for d, (qk_ref, vt_ref, gr_ref, h_ref) in enumerate(dirs):
        vis = col >= row if d == 0 else col <= row
        last = L - 1 if d == 0 else 0
        gr = gr_ref[...]
        cum_r = _lane_cumsum(_logsig(gr[16 * d + 8:16 * d + 16, :]), reverse=d == 1)
        ci_r = gr[16 * d:16 * d + 8, :] - cum_r
        ci_all = jnp.concatenate([ci_r, jnp.zeros((128 - ML_HEADS, L), F32)], axis=0).T
        def decay_and_scores(h):
            idx = d * ML_HEADS + h
            p = h // 2
            q2 = qk_ref[:, p * 128:(p + 1) * 128]
            k2 = qk_ref[:, 512 + p * 128:512 + (p + 1) * 128]
            own = lane >= HEAD_DIM if h % 2 else lane < HEAD_DIM
            qh = jnp.where(own, q2, jnp.zeros_like(q2))
            b_r = cum_r[h:h + 1, :]
            m_prev = m_ref[idx][0:1, 0:1]
            st = st_ref[idx]
            dm = jnp.where(vis, ci_all[:, h:h + 1] + b_r, NEG)
            a_r = b_r + m_prev
            m_r = jnp.maximum(a_r, jnp.max(dm, axis=0, keepdims=True))
            inter = jnp.exp(a_r - m_r)
            sc = lax.dot_general(k2, qh, NT_DIMS, preferred_element_type=F32) * jnp.exp(dm - m_r)
            carry = lax.dot_general(st.astype(BF16), qh, NT_DIMS, preferred_element_type=F32)
            return k2, b_r, m_prev, st, m_r, inter, sc, carry

        pending = decay_and_scores(0)
        for h in range(ML_HEADS):
            idx = d * ML_HEADS + h
            k2, b_r, m_prev, st, m_r, inter, sc, carry = pending
            if h + 1 < ML_HEADS:
                pending = decay_and_scores(h + 1)
            vt = vt_ref[h * ML_V:(h + 1) * ML_V, :]
            i_r = gr[16 * d + h:16 * d + h + 1, :]
            num = inter * carry[0:ML_V, :] + jnp.dot(vt, sc.astype(BF16), preferred_element_type=F32)
            den = inter * carry[ML_V:ML_V + 1, :] + jnp.sum(sc, axis=0, keepdims=True)
            ht = num / jnp.maximum(jnp.abs(den), jnp.exp(-m_r))
            h_ref[:, h * ML_V:(h + 1) * ML_V] = ht.T

            m_new = m_r[:, last:last + 1]
            b_last = b_r[:, last:last + 1]
            g_r = jnp.exp(b_last - b_r + i_r - m_new)
            decay = jnp.exp(b_last + m_prev - m_new)
            gvt = (jnp.concatenate([vt, ones_t], axis=0).astype(F32) * g_r).astype(BF16)
            st_ref[idx] = decay * st + jnp.dot(gvt, k2, preferred_element_type=F32)
            m_ref[idx] = jnp.broadcast_to(m_new, (8, 128))


def _mlstm(qk, vt, gr):
    fwd = lambda b, s: (b, s, 0)
    bwd = lambda b, s: (b, jnp.where(s == 0, 0, NT - s), 0)
    fwd_t = lambda b, s: (b, 0, s)
    bwd_t = lambda b, s: (b, 0, jnp.where(s == 0, 0, NT - s))
    tok = lambda n, ix: pl.BlockSpec((None, TM, n), ix)
    tok_t = lambda n, ix: pl.BlockSpec((None, n, TM), ix)
    return pl.pallas_call(
        _mlstm_kernel,
        grid=(B, NT),
        in_specs=[tok(D, fwd), tok_t(D, fwd_t), tok_t(32, fwd_t),
                  tok(D, bwd), tok_t(D, bwd_t), tok_t(32, bwd_t)],
        out_specs=[tok(D, fwd), tok(D, bwd)],
        out_shape=[jax.ShapeDtypeStruct((B, T, D), F32)] * 2,
        scratch_shapes=[pltpu.VMEM((2 * ML_HEADS, 2 * ML_V, 128), F32),
                        pltpu.VMEM((2 * ML_HEADS, 8, 128), F32)],
        compiler_params=_cparams("parallel", "arbitrary"),
        name="mlstm",
    )(qk, vt, gr, qk, vt, gr)


def _ml_gate_kernel(hf_ref, hb_ref, og_ref, ng_ref, a_ref):
    for h in range(ML_HEADS):
        sl = slice(h * ML_V, (h + 1) * ML_V)
        y = _rms(hf_ref[:, sl] + hb_ref[:, sl], ng_ref[:, sl])
        a_ref[:, sl] = (y * jax.nn.sigmoid(og_ref[:, sl])).astype(BF16)


def _ml_gate(hf, hb, og, ng):
    tile = pl.BlockSpec((None, TM, D), lambda b, i: (b, i, 0))
    return pl.pallas_call(
        _ml_gate_kernel,
        grid=(B, NT),
        in_specs=[tile, tile, tile, _resident((1, D))],
        out_specs=tile,
        out_shape=jax.ShapeDtypeStruct((B, T, D), BF16),
        compiler_params=_cparams("parallel", "parallel"),
        name="ml_gate",
    )(hf, hb, og, ng)


def _ffn_kernel(*refs, has_ctx, nt, final):
    per_tile = 7
    shared = refs[FFN_TPS * per_tile:FFN_TPS * per_tile + 7]
    o_ref = refs[FFN_TPS * per_tile + 7]
    h_scr, x_scr, ua_scr, ug_scr, act_scr = refs[FFN_TPS * per_tile + 8:]
    tiles = []
    for k in range(FFN_TPS):
        tile = pl.program_id(0) * FFN_TPS + k
        tiles.append(_FfnTile(tile % nt, *refs[k * per_tile:(k + 1) * per_tile], *shared,
                              o_ref.at[k * TM:(k + 1) * TM, :], h_scr.at[k], x_scr.at[k], ua_scr.at[k],
                              ug_scr.at[k], act_scr.at[k], has_ctx=has_ctx, nt=nt, final=final))
    nchunk = FFN // FFN_CHUNK
    tiles[0].prologue()
    for k, t in enumerate(tiles):
        for c in range(nchunk):
            if c == nchunk // 2 and k + 1 < FFN_TPS:
                tiles[k + 1].prologue()
            t.chunk(c)
        t.finish()


class _FfnTile:
    def __init__(self, i, a_ref, ap_ref, an_ref, x_ref, xp_ref, xn_ref, mod_ref, wo_ref, g_ref, wup_ref, cw_ref,
                 cb_ref, wdn_ref, fg_ref, o_ref, h_scr, x_scr, ua_scr, ug_scr, act_scr, *, has_ctx, nt, final):
        self.__dict__.update(locals())

    def prologue(self):
        s = self
        mod_ref = s.mod_ref
        g = s.g_ref[...]
        shift, scale, gate1 = mod_ref[3:4, :], mod_ref[4:5, :], mod_ref[2:3, :]
        sub = lax.broadcasted_iota(jnp.int32, (FFN_EDGE, D), 0)
        a_prev = s.ap_ref[...].astype(F32)[FFN_EDGE - 1:FFN_EDGE, :]
        a_next = s.an_ref[...].astype(F32)[0:1, :]
        a_edge = jnp.where(sub == 0, a_prev, jnp.where(sub == 1, a_next, 0.0)).astype(BF16)
        y1 = jnp.dot(jnp.concatenate([s.a_ref[...], a_edge], axis=0), s.wo_ref[...], preferred_element_type=F32)
        x1 = s.x_ref[...] + gate1 * y1[0:TM, :]
        x1_prev = s.xp_ref[HALO - 1:HALO, :] + gate1 * y1[TM:TM + 1, :]
        x1_next = s.xn_ref[0:1, :] + gate1 * y1[TM + 1:TM + 2, :]
        if s.has_ctx:
            prev_ok = s.i >= 2
            next_ok = (s.i != 0) & (s.i != s.nt - 1)
        else:
            prev_ok = s.i >= 1
            next_ok = s.i != s.nt - 1
        x = jnp.swapaxes(x1.reshape(8, TM // 8, D), 0, 1).reshape(TM, D)
        s.x_scr[...] = x
        s.h_scr[0:TM, :] = _norm_mod(x, g, shift, scale)
        h_prev = jnp.where(prev_ok, _norm_mod(x1_prev, g, shift, scale), 0.0)
        h_next = jnp.where(next_ok, _norm_mod(x1_next, g, shift, scale), 0.0)
        s.h_scr[TM:, :] = jnp.where(sub == 0, h_prev, jnp.where(sub == 1, h_next, 0.0))
        s.hb = s.h_scr[...].astype(BF16)

    def _conv(self, scr, off):
        sub8 = lax.broadcasted_iota(jnp.int32, (8, FFN_CHUNK), 0)
        w = self.cw_ref[:, off:off + FFN_CHUNK]
        first = jnp.where(sub8 == 0, scr[TM:TM + 1, :], pltpu.roll(scr[TM - 8:TM, :], 1, 0))
        final_grp = jnp.where(sub8 == 7, scr[TM + 1:TM + 2, :], pltpu.roll(scr[0:8, :], 7, 0))
        prev = jnp.concatenate([first, scr[0:TM - 8, :]], axis=0)
        nxt = jnp.concatenate([scr[8:TM, :], final_grp], axis=0)
        return prev * w[0:1] + scr[0:TM, :] * w[1:2] + nxt * w[2:3] + self.cb_ref[:, off:off + FFN_CHUNK]

    def chunk(self, c):
        s = self
        lo = c * FFN_CHUNK
        s.ua_scr[c % 2] = jnp.dot(s.hb, s.wup_ref[:, lo:lo + FFN_CHUNK], preferred_element_type=F32)
        s.ug_scr[c % 2] = jnp.dot(s.hb, s.wup_ref[:, FFN + lo:FFN + lo + FFN_CHUNK], preferred_element_type=F32)
        a = s._conv(s.ua_scr.at[c % 2], lo)
        gg = s._conv(s.ug_scr.at[c % 2], FFN + lo)
        s.act_scr[:, lo:lo + FFN_CHUNK] = (a * (gg * jax.nn.sigmoid(gg))).astype(BF16)

    def finish(self):
        s = self
        y = jnp.dot(s.act_scr[...], s.wdn_ref[...], preferred_element_type=F32)
        out = s.x_scr[...] + s.mod_ref[5:6, :] * y
        if s.final:
            out = _rms(out, s.fg_ref[...])
        s.o_ref[...] = jnp.swapaxes(out.reshape(TM // 8, 8, D), 0, 1).reshape(TM, D)


def _ffn(a, wo, xs, mod, g, wup, cw, cb, wdn, fg, has_ctx, final):
    rows = a.shape[1]
    nt = rows // TM
    off = 0 if has_ctx else 1
    ntiles = B * nt
    a2 = a.reshape(B * rows, D)
    x2 = xs.reshape(B * T, D)
    x_per_tile = TM // HALO
    a_per_tile = TM // FFN_EDGE
    x_last = B * T // HALO - 1
    a_last = B * rows // FFN_EDGE - 1

    def tile_specs(k):
        ta = lambda j: j * FFN_TPS + k
        tx = lambda j: (ta(j) // nt) * NT + ta(j) % nt + off
        if has_ctx:
            mod_ix = lambda j: (jnp.where(ta(j) % nt == 0, B, ta(j) // nt), 0, 0)
        else:
            mod_ix = lambda j: (ta(j) // nt, 0, 0)
        return [pl.BlockSpec((TM, D), lambda j: (ta(j), 0)),
                pl.BlockSpec((FFN_EDGE, D), lambda j: (jnp.maximum(ta(j) * a_per_tile - 1, 0), 0)),
                pl.BlockSpec((FFN_EDGE, D), lambda j: (jnp.minimum((ta(j) + 1) * a_per_tile, a_last), 0)),
                pl.BlockSpec((TM, D), lambda j: (tx(j), 0)),
                pl.BlockSpec((HALO, D), lambda j: (jnp.maximum(tx(j) * x_per_tile - 1, 0), 0)),
                pl.BlockSpec((HALO, D), lambda j: (jnp.minimum((tx(j) + 1) * x_per_tile, x_last), 0)),
                pl.BlockSpec((None, 6, D), mod_ix)]

    per_tile_specs = [spec for k in range(FFN_TPS) for spec in tile_specs(k)]
    per_tile_args = [arr for _ in range(FFN_TPS) for arr in (a2, a2, a2, x2, x2, x2, mod)]
    out = pl.pallas_call(
        functools.partial(_ffn_kernel, has_ctx=has_ctx, nt=nt, final=final),
        grid=(ntiles // FFN_TPS,),
        in_specs=per_tile_specs + [_resident((D, D)),
                                   _resident((1, D)),
                                   _resident((D, 2 * FFN)),
                                   _resident((3, 2 * FFN)),
                                   _resident((1, 2 * FFN)),
                                   _resident((FFN, D)),
                                   _resident((1, D))],
        out_specs=pl.BlockSpec((FFN_TPS * TM, D), lambda j: (j, 0)),
        out_shape=jax.ShapeDtypeStruct((B * rows, D), F32),
        scratch_shapes=[pltpu.VMEM((FFN_TPS, TM + FFN_EDGE, D), F32),
                        pltpu.VMEM((FFN_TPS, TM, D), F32),
                        pltpu.VMEM((FFN_TPS, 2, TM + FFN_EDGE, FFN_CHUNK), F32),
                        pltpu.VMEM((FFN_TPS, 2, TM + FFN_EDGE, FFN_CHUNK), F32),
                        pltpu.VMEM((FFN_TPS, TM, FFN), BF16)],
        compiler_params=_cparams("parallel"),
        name="ffn",
    )(*per_tile_args, wo, g, wup, cw, cb, wdn, fg)
    return out.reshape(B, rows, D)


def _rope_tables():
    rows = S // GRID_W
    quarter = HEAD_DIM // 4
    row = jnp.repeat(jnp.arange(rows, dtype=F32), GRID_W)
    col = jnp.tile(jnp.arange(GRID_W, dtype=F32), rows)
    inv = 10000.0 ** (-jnp.arange(quarter, dtype=F32) / quarter)
    ang_r = row[:, None] * inv
    ang_c = col[:, None] * inv
    cr, sr, cc, sc = jnp.cos(ang_r), jnp.sin(ang_r), jnp.cos(ang_c), jnp.sin(ang_c)
    z = jnp.zeros_like(sr)
    cos = jnp.concatenate([cr, cr, cc, cc], axis=1)
    sa = jnp.concatenate([-sr, z, -sc, z], axis=1)
    sb = jnp.concatenate([z, sr, z, sc], axis=1)
    pad = lambda t, v: jnp.concatenate([jnp.full((CTX, 128), v, F32), jnp.tile(t, (1, 2))], axis=0)
    return pad(cos, 1.0), pad(sa, 0.0), pad(sb, 0.0)


def _group_major(w, axis):
    shape = w.shape
    w = w.reshape(shape[:axis] + (SW_KV, SW_GRP, HEAD_DIM) + shape[axis + 1:])
    w = jnp.swapaxes(w, axis, axis + 1)
    return w.reshape(shape)


def _row(v):
    return v.reshape(1, -1)


def _bf(w):
    return w.astype(BF16)


def _mod(cpad, w, b):
    return _ada(cpad, w, b).reshape(MOD_ROWS, 6, D)


def _da_layer(idx, xs, mod, rope, norm1_g, w_in, lq1, lk1, lq2, lk2, subln_g, w_out):
    need_ctx = idx < DEPTH - 1
    lam_init = 0.8 - 0.6 * math.exp(-0.3 * idx)
    qkv = _attn_proj(xs, mod, _row(norm1_g), _bf(w_in), rope, 4, 4, 4)
    o = _da_attn(qkv, jnp.stack([lq1, lk1, lq2, lk2]), _row(subln_g), lam_init, need_ctx)
    return o, _bf(w_out)


def _ml_layer(xs, mod, norm1_g, w_in, gate_b, norm_g, w_out):
    qk, vt, og, gr = _ml_proj(xs, mod, _row(norm1_g), _bf(w_in[:, :3072]), _bf(w_in[:, 3072:]).T,
                              gate_b.reshape(-1, 1))
    hf, hb = _mlstm(qk, vt, gr)
    return _ml_gate(hf, hb, og, _row(norm_g)), _bf(w_out)


def _sw_layer(xs, mod, rope, norm1_g, w_in, sink, w_out):
    w_q = _group_major(w_in[:, :D], 1)
    qkv = _attn_proj(xs, mod, _row(norm1_g), _bf(jnp.concatenate([w_q, w_in[:, D:]], axis=1)), rope, 4, 1, 1)
    o = _swa_attn(qkv, _row(sink))
    return o, _bf(_group_major(w_out, 0))


def _ffn_layer(idx, mix, xs, mod, norm2_g, w_up, conv_w, conv_b, w_down, final_norm_g):
    final = idx == DEPTH - 1
    return _ffn(mix[0], mix[1], xs, mod, _row(norm2_g), _bf(w_up), conv_w, _row(conv_b), _bf(w_down),
                _row(final_norm_g), has_ctx=not final, final=final)


def kernel(x, c, ctx, c_ctx, l0_ada_w, l0_ada_b, l0_norm1_g, l0_da_w_in, l0_da_lam_q1, l0_da_lam_k1, l0_da_lam_q2, l0_da_lam_k2, l0_da_subln_g, l0_da_w_out, l0_norm2_g, l0_ffn_w_up, l0_ffn_conv_w, l0_ffn_conv_b, l0_ffn_w_down, l1_ada_w, l1_ada_b, l1_norm1_g, l1_ml_w_in, l1_ml_gate_b, l1_ml_norm_g, l1_ml_w_out, l1_norm2_g, l1_ffn_w_up, l1_ffn_conv_w, l1_ffn_conv_b, l1_ffn_w_down, l2_ada_w, l2_ada_b, l2_norm1_g, l2_sw_w_in, l2_sw_sink, l2_sw_w_out, l2_norm2_g, l2_ffn_w_up, l2_ffn_conv_w, l2_ffn_conv_b, l2_ffn_w_down, l3_ada_w, l3_ada_b, l3_norm1_g, l3_da_w_in, l3_da_lam_q1, l3_da_lam_k1, l3_da_lam_q2, l3_da_lam_k2, l3_da_subln_g, l3_da_w_out, l3_norm2_g, l3_ffn_w_up, l3_ffn_conv_w, l3_ffn_conv_b, l3_ffn_w_down, final_norm_g):
    rope = _rope_tables()
    xs = jnp.concatenate([ctx, x], axis=1)
    cpad = jnp.concatenate([c, c_ctx[None, :], jnp.zeros((MOD_ROWS - B - 1, D), F32)], axis=0)

    m = _mod(cpad, l0_ada_w, l0_ada_b)
    mix = _da_layer(0, xs, m, rope, l0_norm1_g, l0_da_w_in, l0_da_lam_q1, l0_da_lam_k1, l0_da_lam_q2, l0_da_lam_k2,
                    l0_da_subln_g, l0_da_w_out)
    xs = _ffn_layer(0, mix, xs, m, l0_norm2_g, l0_ffn_w_up, l0_ffn_conv_w, l0_ffn_conv_b, l0_ffn_w_down, final_norm_g)

    m = _mod(cpad, l1_ada_w, l1_ada_b)
    mix = _ml_layer(xs, m, l1_norm1_g, l1_ml_w_in, l1_ml_gate_b, l1_ml_norm_g, l1_ml_w_out)
    xs = _ffn_layer(1, mix, xs, m, l1_norm2_g, l1_ffn_w_up, l1_ffn_conv_w, l1_ffn_conv_b, l1_ffn_w_down, final_norm_g)

    m = _mod(cpad, l2_ada_w, l2_ada_b)
    mix = _sw_layer(xs, m, rope, l2_norm1_g, l2_sw_w_in, l2_sw_sink, l2_sw_w_out)
    xs = _ffn_layer(2, mix, xs, m, l2_norm2_g, l2_ffn_w_up, l2_ffn_conv_w, l2_ffn_conv_b, l2_ffn_w_down, final_norm_g)

    m = _mod(cpad, l3_ada_w, l3_ada_b)
    mix = _da_layer(3, xs, m, rope, l3_norm1_g, l3_da_w_in, l3_da_lam_q1, l3_da_lam_k1, l3_da_lam_q2, l3_da_lam_k2,
                    l3_da_subln_g, l3_da_w_out)
    return _ffn_layer(3, mix, xs, m, l3_norm2_g, l3_ffn_w_up, l3_ffn_conv_w, l3_ffn_conv_b, l3_ffn_w_down,
                      final_norm_g)
```
